```python
import jax
import jax.numpy as jnp
from jax import lax
import numpy as np

D_MODEL = 1024
BATCH = 16
SEQ = 2048
DEPTH = 2

GRID_W = 64
CTX_LEN = 256

HEAD_DIM = 64
N_Q_HEADS = 8
N_KV_HEADS = 2
Q_PER_KV = N_Q_HEADS // N_KV_HEADS
ATTN_W = N_Q_HEADS * HEAD_DIM
KV_W = N_KV_HEADS * HEAD_DIM
ROT_FREQS = HEAD_DIM // 4
ROPE_THETA = 10000.0
Q_BLOCK = 128
CONV_W = D_MODEL // 4
CONV_K = 3
FFT_W = D_MODEL // 4
FFT_GROUPS = 4
FFT_GROUP_DIM = FFT_W // FFT_GROUPS
MIX_W = ATTN_W + CONV_W + FFT_W
Q_OFF = 0
K_OFF = Q_OFF + ATTN_W
V_OFF = K_OFF + KV_W
CB_OFF = V_OFF + KV_W
CC_OFF = CB_OFF + CONV_W
CV_OFF = CC_OFF + CONV_W
F_OFF = CV_OFF + CONV_W
IN_W = F_OFF + FFT_W
N_EXPERTS = 32
TOP_K = 4
D_FF = D_MODEL
SWIGLU_LIMIT = 7.0
SWIGLU_ALPHA = 1.702
N_MOD = 6
EPS = 1e-6

kernel_name = "hybrid_fourier_conv_gqa_moe_dit_block"


def rms_norm(x, g=None):
    xf = x.astype(jnp.float32)
    y = xf * lax.rsqrt(jnp.mean(xf * xf, axis=-1, keepdims=True) + EPS)
    if g is not None:
        y = y * g.astype(jnp.float32)
    return y.astype(x.dtype)


def modulate(x, g, shift, scale):
    return rms_norm(x, g) * (1 + scale) + shift


def rope_tables(rows):
    row = jnp.repeat(jnp.arange(rows), GRID_W)
    col = jnp.tile(jnp.arange(GRID_W), rows)
    inv_freq = ROPE_THETA ** (-jnp.arange(ROT_FREQS, dtype=jnp.float32) / ROT_FREQS)
    ang = jnp.stack([row, col], axis=-1).astype(jnp.float32)[..., None] * inv_freq
    return jnp.cos(ang), jnp.sin(ang)


def apply_rope(x, cos, sin):
    shp = x.shape
    xr = x.astype(jnp.float32).reshape(shp[:-1] + (2, 2, ROT_FREQS))
    x1, x2 = xr[..., 0, :], xr[..., 1, :]
    bshape = (1, shp[1]) + (1,) * (x.ndim - 3) + (2, ROT_FREQS)
    c = cos.reshape(bshape)
    s = sin.reshape(bshape)
    out = jnp.stack([x1 * c - x2 * s, x2 * c + x1 * s], axis=-2)
    return out.reshape(shp).astype(x.dtype)


def keys_values(p_kv, g_k):
    b, n, _ = p_kv.shape
    k = rms_norm(p_kv[..., :KV_W].reshape(b, n, N_KV_HEADS, HEAD_DIM), g_k)
    v = p_kv[..., KV_W:].reshape(b, n, N_KV_HEADS, HEAD_DIM)
    return k, v


def attend(q, k, v):
    b, nq = q.shape[:2]
    nb = nq // Q_BLOCK
    scale = HEAD_DIM ** -0.5
    qb = q.reshape(b, nb, Q_BLOCK, N_KV_HEADS, Q_PER_KV, HEAD_DIM).transpose(1, 0, 2, 3, 4, 5)

    def one_block(qi):
        s = jnp.einsum('bqhgd,bkhd->bhgqk', qi, k, preferred_element_type=jnp.float32) * scale
        p = jax.nn.softmax(s, axis=-1)
        return jnp.einsum('bhgqk,bkhd->bqhgd', p.astype(v.dtype), v)

    o = lax.map(one_block, qb)
    return o.transpose(1, 0, 2, 3, 4, 5).reshape(b, nq, ATTN_W)


def short_conv(u, w):
    up = jnp.pad(u, ((0, 0), (1, 1), (0, 0)))
    return up[:, :-2] * w[0] + up[:, 1:-1] * w[1] + up[:, 2:] * w[2]


def fourier_mix(f):
    b, n, _ = f.shape
    fg = f.astype(jnp.float32).reshape(b, n, FFT_GROUPS, FFT_GROUP_DIM)
    y = jnp.fft.fft2(fg, axes=(1, 3)).real
    return y.reshape(b, n, FFT_W).astype(f.dtype)


def mixers(p, k, v, g_q, conv_w, g_branch, w_out, rope):
    b, n, _ = p.shape
    q = rms_norm(p[..., Q_OFF:K_OFF].reshape(b, n, N_KV_HEADS, Q_PER_KV, HEAD_DIM), g_q)
    if rope is not None:
        q = apply_rope(q, rope[0], rope[1])
    attn = attend(q, k, v)
    gate_b = p[..., CB_OFF:CC_OFF]
    gate_c = p[..., CC_OFF:CV_OFF]
    conv_in = p[..., CV_OFF:F_OFF]
    conv = gate_b * short_conv(gate_c * conv_in, conv_w)
    fourier = fourier_mix(p[..., F_OFF:IN_W])
    merged = jnp.concatenate([rms_norm(attn), rms_norm(conv), rms_norm(fourier)], axis=-1) * g_branch
    return merged @ w_out


def moe_ffn(h, w_r, b_r, w_gu, b_gu, w_dn, b_dn):
    shp = h.shape
    t = h.reshape(-1, shp[-1])
    logits = t.astype(jnp.float32) @ w_r.astype(jnp.float32) + b_r.astype(jnp.float32)
    top_v, top_i = lax.top_k(logits, TOP_K)
    top_w = jax.nn.softmax(top_v, axis=-1)
    gates = jnp.sum(jax.nn.one_hot(top_i, N_EXPERTS, dtype=jnp.float32) * top_w[..., None], axis=1)
    out = jnp.zeros(t.shape, jnp.float32)
    for e in range(N_EXPERTS):
        gu = t @ w_gu[e] + b_gu[e]
        a = jnp.minimum(gu[..., :D_FF], SWIGLU_LIMIT)
        lin = jnp.clip(gu[..., D_FF:], -SWIGLU_LIMIT, SWIGLU_LIMIT)
        y = (a * jax.nn.sigmoid(SWIGLU_ALPHA * a) * (lin + 1)) @ w_dn[e] + b_dn[e]
        out = out + gates[:, e:e + 1] * y.astype(jnp.float32)
    return out.astype(h.dtype).reshape(shp)


def setup_inputs(seed: int = 0) -> dict:
    key = jax.random.key(seed)
    ks = jax.random.split(key, 24)
    f32 = jnp.float32
    L, D, E = DEPTH, D_MODEL, N_EXPERTS

    def nrm(k, shape, scale):
        return jax.random.normal(k, shape, f32) * scale

    def gain(k, shape):
        return 1.0 + 0.05 * jax.random.normal(k, shape, f32)

    return {
        "x": nrm(ks[0], (BATCH, SEQ, D), 1.0),
        "c": nrm(ks[1], (BATCH, D), 1.0),
        "ctx": nrm(ks[2], (BATCH, CTX_LEN, D), 1.0),
        "c_ctx": nrm(ks[3], (D,), 1.0),
        "w_mod": nrm(ks[4], (L, D, N_MOD * D), 0.5 * D ** -0.5),
        "b_mod": nrm(ks[5], (L, N_MOD * D), 0.02),
        "g_pre_mix": gain(ks[6], (L, D)),
        "g_post_mix": gain(ks[7], (L, D)),
        "g_pre_ffn": gain(ks[8], (L, D)),
        "g_post_ffn": gain(ks[9], (L, D)),
        "w_in": nrm(ks[10], (L, D, IN_W), D ** -0.5),
        "g_q": gain(ks[11], (L, HEAD_DIM)),
        "g_k": gain(ks[12], (L, HEAD_DIM)),
        "conv_w": nrm(ks[13], (L, CONV_K, CONV_W), CONV_K ** -0.5),
        "g_branch": gain(ks[14], (L, MIX_W)),
        "w_out": nrm(ks[15], (L, MIX_W, D), MIX_W ** -0.5),
        "w_router": nrm(ks[16], (L, D, E), D ** -0.5),
        "b_router": nrm(ks[17], (L, E), 0.01),
        "w_gate_up": nrm(ks[18], (L, E, D, 2 * D_FF), D ** -0.5),
        "b_gate_up": nrm(ks[19], (L, E, 2 * D_FF), 0.02),
        "w_down": nrm(ks[20], (L, E, D_FF, D), D_FF ** -0.5),
        "b_down": nrm(ks[21], (L, E, D), 0.02),
    }


def reference(x, c, ctx, c_ctx, w_mod, b_mod, g_pre_mix, g_post_mix, g_pre_ffn, g_post_ffn,
              w_in, g_q, g_k, conv_w, g_branch, w_out, w_router, b_router,
              w_gate_up, b_gate_up, w_down, b_down):
    n_lat = x.shape[1]
    rows = n_lat // GRID_W
    cos, sin = rope_tables(rows)
    silu_c = jax.nn.silu(c)
    silu_cc = jax.nn.silu(c_ctx)
    x_lat, x_ctx = x, ctx
    for layer in range(DEPTH):
        last = layer == DEPTH - 1
        mod_lat = (silu_c @ w_mod[layer] + b_mod[layer])[:, None, :]
        mod_ctx = silu_cc @ w_mod[layer] + b_mod[layer]
        sh_m, sc_m, gt_m, sh_f, sc_f, gt_f = jnp.split(mod_lat, N_MOD, axis=-1)
        csh_m, csc_m, cgt_m, csh_f, csc_f, cgt_f = jnp.split(mod_ctx, N_MOD, axis=-1)

        h_lat = modulate(x_lat, g_pre_mix[layer], sh_m, sc_m)
        h_ctx = modulate(x_ctx, g_pre_mix[layer], csh_m, csc_m)
        w_in_l = w_in[layer]
        p_lat = h_lat @ w_in_l
        if last:
            k_ctx, v_ctx = keys_values(h_ctx @ w_in_l[:, K_OFF:CB_OFF], g_k[layer])
        else:
            p_ctx = h_ctx @ w_in_l
            k_ctx, v_ctx = keys_values(p_ctx[..., K_OFF:CB_OFF], g_k[layer])
        k_lat, v_lat = keys_values(p_lat[..., K_OFF:CB_OFF], g_k[layer])
        k_lat = apply_rope(k_lat, cos, sin)
        k_all = jnp.concatenate([k_lat, k_ctx], axis=1)
        v_all = jnp.concatenate([v_lat, v_ctx], axis=1)
        mix_lat = mixers(p_lat, k_all, v_all, g_q[layer], conv_w[layer], g_branch[layer],
                         w_out[layer], (cos, sin))
        x_lat = x_lat + gt_m * rms_norm(mix_lat, g_post_mix[layer])

        if last:
            f_lat = modulate(x_lat, g_pre_ffn[layer], sh_f, sc_f)
            y_lat = moe_ffn(f_lat, w_router[layer], b_router[layer], w_gate_up[layer],
                            b_gate_up[layer], w_down[layer], b_down[layer])
            x_lat = x_lat + gt_f * rms_norm(y_lat, g_post_ffn[layer])
        else:
            mix_ctx = mixers(p_ctx, k_ctx, v_ctx, g_q[layer], conv_w[layer], g_branch[layer],
                             w_out[layer], None)
            x_ctx = x_ctx + cgt_m * rms_norm(mix_ctx, g_post_mix[layer])
            f_lat = modulate(x_lat, g_pre_ffn[layer], sh_f, sc_f)
            f_ctx = modulate(x_ctx, g_pre_ffn[layer], csh_f, csc_f)
            y_all = moe_ffn(jnp.concatenate([f_lat, f_ctx], axis=1), w_router[layer], b_router[layer],
                            w_gate_up[layer], b_gate_up[layer], w_down[layer], b_down[layer])
            x_lat = x_lat + gt_f * rms_norm(y_all[:, :n_lat], g_post_ffn[layer])
            x_ctx = x_ctx + cgt_f * rms_norm(y_all[:, n_lat:], g_post_ffn[layer])
    return x_lat
```

```python
import functools
import math

import jax
import jax.numpy as jnp
from jax import lax
from jax.experimental import pallas as pl
from jax.experimental.pallas import tpu as pltpu

D_MODEL = 1024
GRID_W = 64
HEAD_DIM = 64
N_Q_HEADS = 8
N_KV_HEADS = 2
Q_PER_KV = N_Q_HEADS // N_KV_HEADS
ATTN_W = N_Q_HEADS * HEAD_DIM
KV_W = N_KV_HEADS * HEAD_DIM
ROT_FREQS = HEAD_DIM // 4
ROPE_THETA = 10000.0
CONV_W = D_MODEL // 4
CONV_K = 3
FFT_W = D_MODEL // 4
FFT_GROUPS = 4
FFT_GROUP_DIM = FFT_W // FFT_GROUPS
MIX_W = ATTN_W + CONV_W + FFT_W
Q_OFF = 0
K_OFF = Q_OFF + ATTN_W
V_OFF = K_OFF + KV_W
CB_OFF = V_OFF + KV_W
CC_OFF = CB_OFF + CONV_W
CV_OFF = CC_OFF + CONV_W
F_OFF = CV_OFF + CONV_W
IN_W = F_OFF + FFT_W
N_EXPERTS = 32
TOP_K = 4
D_FF = D_MODEL
SWIGLU_LIMIT = 7.0
SWIGLU_ALPHA = 1.702
N_MOD = 6
EPS = 1e-6

LANES = 128
SUBLANES = 8
ROW_CHUNKS = D_MODEL // LANES
MOD_ROWS_PAD = 24
HALO_ROWS = 16

F32 = jnp.float32
BF16 = jnp.bfloat16
HIGHEST = lax.Precision.HIGHEST

ROW_TILE = 256
MOE_ROW_TILE = 256
MOE_STAGE_STRIDE = MOE_ROW_TILE + SUBLANES
SCATTER_BATCH = 4


def _vmem_limit(mib):
    return pltpu.CompilerParams(vmem_limit_bytes=mib * 1024 * 1024)


def _dot(a, b):
    return jnp.dot(a, b, preferred_element_type=F32)


def _mod_kernel(cc_ref, w_ref, b_ref, o_ref):
    cc = cc_ref[...]
    s = cc * (1.0 / (1.0 + jnp.exp(-cc)))
    o_ref[0] = jnp.dot(s, w_ref[0], preferred_element_type=F32, precision=HIGHEST) + b_ref[0]


def _modulation(cc, w_mod, b_mod):
    depth = w_mod.shape[0]
    tn = 1024
    return pl.pallas_call(
        _mod_kernel,
        grid=(depth, N_MOD * D_MODEL // tn),
        in_specs=[
            pl.BlockSpec((MOD_ROWS_PAD, D_MODEL), lambda l, j: (0, 0)),
            pl.BlockSpec((1, D_MODEL, tn), lambda l, j: (l, 0, j)),
            pl.BlockSpec((1, 1, tn), lambda l, j: (l, 0, j)),
        ],
        out_specs=pl.BlockSpec((1, MOD_ROWS_PAD, tn), lambda l, j: (l, 0, j)),
        out_shape=jax.ShapeDtypeStruct((depth, MOD_ROWS_PAD, N_MOD * D_MODEL), F32),
        compiler_params=_vmem_limit(32),
    )(cc, w_mod, b_mod.reshape(depth, 1, N_MOD * D_MODEL))


def _modulated_norm(x, gain, shift, scale):
    ms = jnp.mean(x * x, axis=-1, keepdims=True)
    return x * lax.rsqrt(ms + EPS) * gain * (1.0 + scale) + shift


def _head_pair_norm_rope(xc, gain, cos, sin):
    lane = lax.broadcasted_iota(jnp.int32, xc.shape, 1)
    lo_head = lane < HEAD_DIM
    x2 = xc * xc
    s_all = jnp.sum(x2, axis=-1, keepdims=True)
    s_lo = jnp.sum(jnp.where(lo_head, x2, 0.0), axis=-1, keepdims=True)
    s_hi = s_all - s_lo
    inv = jnp.where(lo_head, lax.rsqrt(s_lo * (1.0 / HEAD_DIM) + EPS), lax.rsqrt(s_hi * (1.0 / HEAD_DIM) + EPS))
    y = xc * inv * gain
    first_half = (lane % (2 * ROT_FREQS)) < ROT_FREQS
    partner = jnp.where(first_half, pltpu.roll(y, LANES - ROT_FREQS, axis=1), pltpu.roll(y, ROT_FREQS, axis=1))
    return y * cos + partner * sin


def _inproj_kernel(x_ref, mod_ref, gpre_ref, w_ref, cos_ref, sin_ref, gq_ref, gk_ref, bdc_ref, bds_ref,
                   *out_refs, kv_only):
    mod = mod_ref[0]
    h = _modulated_norm(x_ref[...], gpre_ref[...], mod[:, 0:D_MODEL], mod[:, D_MODEL:2 * D_MODEL])
    p = _dot(h.astype(BF16), w_ref[...])
    cos = cos_ref[...]
    sin = sin_ref[...]
    if kv_only:
        k_ref, v_ref = out_refs
        k_off, v_off = 0, KV_W
    else:
        q_ref, k_ref, v_ref, gb_ref, u_ref, fcs_ref = out_refs
        k_off, v_off = K_OFF, V_OFF
        for c in range(ATTN_W // LANES):
            qc = _head_pair_norm_rope(p[:, c * LANES:(c + 1) * LANES], gq_ref[...], cos, sin)
            q_ref[:, c * LANES:(c + 1) * LANES] = (qc * (HEAD_DIM ** -0.5)).astype(BF16)
        gb_ref[...] = p[:, CB_OFF:CC_OFF].astype(BF16)
        u_ref[...] = (p[:, CC_OFF:CV_OFF] * p[:, CV_OFF:F_OFF]).astype(BF16)
        f = p[:, F_OFF:IN_W].astype(BF16)
        fcs_ref[:, 0:FFT_W] = _dot(f, bdc_ref[...]).astype(BF16)
        fcs_ref[:, FFT_W:2 * FFT_W] = _dot(f, bds_ref[...]).astype(BF16)
    k_ref[...] = _head_pair_norm_rope(p[:, k_off:k_off + KV_W], gk_ref[...], cos, sin).astype(BF16)
    v_ref[...] = p[:, v_off:v_off + KV_W].astype(BF16)


def _inproj(x, mods3, mod_row_fn, gpre, w, cos, sin, gq2, gk2, bdc, bds, seq, kv_only):
    t = x.shape[0]
    tm = ROW_TILE
    tiles_per_seq = seq // tm
    width = w.shape[1]
    row = lambda i: (i, 0)
    const = lambda i: (0, 0)
    in_specs = [
        pl.BlockSpec((tm, D_MODEL), row),
        pl.BlockSpec((1, 1, N_MOD * D_MODEL), lambda i: (mod_row_fn(i // tiles_per_seq), 0, 0)),
        pl.BlockSpec((1, D_MODEL), const),
        pl.BlockSpec((D_MODEL, width), const),
        pl.BlockSpec((tm, LANES), lambda i: (i % tiles_per_seq, 0)),
        pl.BlockSpec((tm, LANES), lambda i: (i % tiles_per_seq, 0)),
        pl.BlockSpec((1, LANES), const),
        pl.BlockSpec((1, LANES), const),
        pl.BlockSpec((FFT_W, FFT_W), const),
        pl.BlockSpec((FFT_W, FFT_W), const),
    ]
    kv_shapes = [jax.ShapeDtypeStruct((t, KV_W), BF16)] * 2
    kv_specs = [pl.BlockSpec((tm, KV_W), row)] * 2
    if kv_only:
        out_shape, out_specs = kv_shapes, kv_specs
    else:
        out_shape = ([jax.ShapeDtypeStruct((t, ATTN_W), BF16)] + kv_shapes
                     + [jax.ShapeDtypeStruct((t, CONV_W), BF16)] * 2 + [jax.ShapeDtypeStruct((t, 2 * FFT_W), BF16)])
        out_specs = ([pl.BlockSpec((tm, ATTN_W), row)] + kv_specs
                     + [pl.BlockSpec((tm, CONV_W), row)] * 2 + [pl.BlockSpec((tm, 2 * FFT_W), row)])
    return pl.pallas_call(
        functools.partial(_inproj_kernel, kv_only=kv_only),
        grid=(t // tm,),
        in_specs=in_specs,
        out_specs=out_specs,
        out_shape=out_shape,
        compiler_params=_vmem_limit(48),
    )(x, mods3, gpre, w, cos, sin, gq2, gk2, bdc, bds)


def _attn_kernel(q_ref, k_ref, v_ref, *rest, has_ctx):
    if has_ctx:
        kc_ref, vc_ref, o_ref = rest
    else:
        (o_ref,) = rest
    nt_dims = (((1,), (1,)), ((), ()))
    for h in range(N_Q_HEADS):
        kv = h // Q_PER_KV
        qh = q_ref[:, h * HEAD_DIM:(h + 1) * HEAD_DIM]
        kh = k_ref[:, kv * HEAD_DIM:(kv + 1) * HEAD_DIM]
        vh = v_ref[:, kv * HEAD_DIM:(kv + 1) * HEAD_DIM]
        s = lax.dot_general(qh, kh, nt_dims, preferred_element_type=F32)
        m = jnp.max(s, axis=-1, keepdims=True)
        if has_ctx:
            kch = kc_ref[:, kv * HEAD_DIM:(kv + 1) * HEAD_DIM]
            vch = vc_ref[:, kv * HEAD_DIM:(kv + 1) * HEAD_DIM]
            s2 = lax.dot_general(qh, kch, nt_dims, preferred_element_type=F32)
            m = jnp.maximum(m, jnp.max(s2, axis=-1, keepdims=True))
        p = jnp.exp(s - m)
        l = jnp.sum(p, axis=-1, keepdims=True)
        o = _dot(p.astype(BF16), vh)
        if has_ctx:
            p2 = jnp.exp(s2 - m)
            l = l + jnp.sum(p2, axis=-1, keepdims=True)
            o = o + _dot(p2.astype(BF16), vch)
        o_ref[:, h * HEAD_DIM:(h + 1) * HEAD_DIM] = (o / l).astype(BF16)


def _attention(q, k, v, seq, kc=None, vc=None, ctx_len=None):
    t = q.shape[0]
    tq = ROW_TILE
    tiles_per_seq = seq // tq
    has_ctx = kc is not None
    in_specs = [
        pl.BlockSpec((tq, ATTN_W), lambda b, i: (b * tiles_per_seq + i, 0)),
        pl.BlockSpec((seq, KV_W), lambda b, i: (b, 0)),
        pl.BlockSpec((seq, KV_W), lambda b, i: (b, 0)),
    ]
    args = [q, k, v]
    if has_ctx:
        in_specs += [pl.BlockSpec((ctx_len, KV_W), lambda b, i: (b, 0))] * 2
        args += [kc, vc]
    return pl.pallas_call(
        functools.partial(_attn_kernel, has_ctx=has_ctx),
        grid=(t // seq, tiles_per_seq),
        in_specs=in_specs,
        out_specs=pl.BlockSpec((tq, ATTN_W), lambda b, i: (b * tiles_per_seq + i, 0)),
        out_shape=jax.ShapeDtypeStruct((t, ATTN_W), BF16),
        compiler_params=_vmem_limit(48),
    )(*args)


def _dft_kernel(cn_ref, sn_ref, fcs_ref, o_ref):
    y = _dot(cn_ref[...], fcs_ref[:, 0:FFT_W]) - _dot(sn_ref[...], fcs_ref[:, FFT_W:2 * FFT_W])
    o_ref[...] = y.astype(BF16)


def _position_dft(cn, sn, fcs, seq):
    t = fcs.shape[0]
    tn = min(seq, 1024)
    tiles_per_seq = seq // tn
    return pl.pallas_call(
        _dft_kernel,
        grid=(tiles_per_seq, t // seq),
        in_specs=[
            pl.BlockSpec((tn, seq), lambda i, b: (i, 0)),
            pl.BlockSpec((tn, seq), lambda i, b: (i, 0)),
            pl.BlockSpec((seq, 2 * FFT_W), lambda i, b: (b, 0)),
        ],
        out_specs=pl.BlockSpec((tn, FFT_W), lambda i, b: (b * tiles_per_seq + i, 0)),
        out_shape=jax.ShapeDtypeStruct((t, FFT_W), BF16),
        compiler_params=_vmem_limit(48),
    )(cn, sn, fcs)


def _rms(x):
    return x * lax.rsqrt(jnp.mean(x * x, axis=-1, keepdims=True) + EPS)


def _split_bf16(x):
    hi = x.astype(BF16)
    return hi, (x - hi.astype(F32)).astype(BF16)


def _merge_kernel(o_ref, gb_ref, u_ref, up_ref, un_ref, four_ref, x_ref, mod_ref, cw_ref, gbr_ref, wout_ref,
                  gpost_ref, gffn_ref, wrh_ref, wrl_ref, br_ref,
                  xn_ref, f_ref, ti_ref, tw_ref, *, tiles_per_seq):
    i = pl.program_id(0)
    tm = x_ref.shape[0]
    mod = mod_ref[0]
    u = u_ref[...].astype(F32)
    rowi = lax.broadcasted_iota(jnp.int32, u.shape, 0)
    first = (i % tiles_per_seq) == 0
    last = (i % tiles_per_seq) == tiles_per_seq - 1
    prev_row = jnp.where(first, 0.0, up_ref[HALO_ROWS - 1:HALO_ROWS, :].astype(F32))
    next_row = jnp.where(last, 0.0, un_ref[0:1, :].astype(F32))
    u_up = jnp.where(rowi == 0, prev_row, pltpu.roll(u, 1, axis=0))
    u_dn = jnp.where(rowi == tm - 1, next_row, pltpu.roll(u, tm - 1, axis=0))
    conv = gb_ref[...].astype(F32) * (u_up * cw_ref[0:1, :] + u * cw_ref[1:2, :] + u_dn * cw_ref[2:3, :])
    merged = jnp.concatenate(
        [_rms(o_ref[...].astype(F32)), _rms(conv), _rms(four_ref[...].astype(F32))], axis=-1) * gbr_ref[...]
    mix = _dot(merged.astype(BF16), wout_ref[...])
    xn = x_ref[...] + mod[:, 2 * D_MODEL:3 * D_MODEL] * (_rms(mix) * gpost_ref[...])
    xn_ref[...] = xn
    f = _modulated_norm(xn, gffn_ref[...], mod[:, 3 * D_MODEL:4 * D_MODEL], mod[:, 4 * D_MODEL:5 * D_MODEL])
    for j in range(ROW_CHUNKS):
        f_ref[pl.ds(j, tm, stride=ROW_CHUNKS), :] = f[:, j * LANES:(j + 1) * LANES]
    f_hi, f_lo = _split_bf16(f)
    logits = _dot(f_hi, wrh_ref[...]) + _dot(f_lo, wrh_ref[...]) + _dot(f_hi, wrl_ref[...]) + br_ref[...]
    lane = lax.broadcasted_iota(jnp.int32, logits.shape, 1)
    neg = jnp.float32(-jnp.inf)
    work = jnp.where(lane < N_EXPERTS, logits, neg)
    top_v, top_i = [], []
    for _ in range(TOP_K):
        m = jnp.max(work, axis=-1, keepdims=True)
        idx = jnp.min(jnp.where(work == m, lane, LANES), axis=-1, keepdims=True)
        top_v.append(m)
        top_i.append(idx)
        work = jnp.where(lane == idx, neg, work)
    ex = [jnp.exp(v - top_v[0]) for v in top_v]
    den = ex[0] + ex[1] + ex[2] + ex[3]
    ti = jnp.zeros(logits.shape, jnp.int32)
    tw = jnp.zeros(logits.shape, F32)
    for kk in range(TOP_K):
        ti = jnp.where(lane == kk, top_i[kk], ti)
        tw = jnp.where(lane == kk, ex[kk] / den, tw)
    ti_ref[...] = ti
    tw_ref[...] = tw


def _merge(o, gb, u, four, x, mods3, mod_row_fn, conv_w, gbr, wout, gpost, gffn, wrh, wrl, br, seq):
    t = x.shape[0]
    tm = ROW_TILE
    tiles_per_seq = seq // tm
    sub_per_tile = tm // HALO_ROWS
    n_sub = t // HALO_ROWS
    row = lambda i: (i, 0)
    const = lambda i: (0, 0)
    in_specs = [
        pl.BlockSpec((tm, ATTN_W), row),
        pl.BlockSpec((tm, CONV_W), row),
        pl.BlockSpec((tm, CONV_W), row),
        pl.BlockSpec((HALO_ROWS, CONV_W), lambda i: (jnp.maximum(i * sub_per_tile - 1, 0), 0)),
        pl.BlockSpec((HALO_ROWS, CONV_W), lambda i: (jnp.minimum((i + 1) * sub_per_tile, n_sub - 1), 0)),
        pl.BlockSpec((tm, FFT_W), row),
        pl.BlockSpec((tm, D_MODEL), row),
        pl.BlockSpec((1, 1, N_MOD * D_MODEL), lambda i: (mod_row_fn(i // tiles_per_seq), 0, 0)),
        pl.BlockSpec((CONV_K, CONV_W), const),
        pl.BlockSpec((1, MIX_W), const),
        pl.BlockSpec((MIX_W, D_MODEL), const),
        pl.BlockSpec((1, D_MODEL), const),
        pl.BlockSpec((1, D_MODEL), const),
        pl.BlockSpec((D_MODEL, LANES), const),
        pl.BlockSpec((D_MODEL, LANES), const),
        pl.BlockSpec((1, LANES), const),
    ]
    out_shape = [
        jax.ShapeDtypeStruct((t, D_MODEL), F32),
        jax.ShapeDtypeStruct((t * ROW_CHUNKS, LANES), F32),
        jax.ShapeDtypeStruct((t, LANES), jnp.int32),
        jax.ShapeDtypeStruct((t, LANES), F32),
    ]
    out_specs = [
        pl.BlockSpec((tm, D_MODEL), row),
        pl.BlockSpec((tm * ROW_CHUNKS, LANES), row),
        pl.BlockSpec((tm, LANES), row),
        pl.BlockSpec((tm, LANES), row),
    ]
    return pl.pallas_call(
        functools.partial(_merge_kernel, tiles_per_seq=tiles_per_seq),
        grid=(t // tm,),
        in_specs=in_specs,
        out_specs=out_specs,
        out_shape=out_shape,
        compiler_params=_vmem_limit(48),
    )(o, gb, u, u, u, four, x, mods3, conv_w, gbr, wout, gpost, gffn, wrh, wrl, br)


def _moe_kernel(cnt_ref, off_ref, idx_ref, wl_ref, src_ref, wgu_ref, bgu_ref, wdn_ref, bdn_ref, out_ref,
                xt_ref, yt_ref, *, tb):
    s = pl.program_id(0)
    e = pl.program_id(1)
    tmr = MOE_ROW_TILE
    ss = MOE_STAGE_STRIDE

    @pl.when(e == 0)
    def _():
        out_ref[...] = jnp.zeros(out_ref.shape, F32)

    cnt = cnt_ref[s * N_EXPERTS + e]
    off = off_ref[s * N_EXPERTS + e]

    def tile(t, carry):
        base = off + t * tmr
        rem = cnt - t * tmr
        for mi in range(tmr):
            tok = idx_ref[0, 0, base + mi]
            slab = src_ref[pl.ds(pl.multiple_of(tok * ROW_CHUNKS, ROW_CHUNKS), ROW_CHUNKS), :]
            xt_ref[pl.ds(mi, ROW_CHUNKS, stride=ss), :] = slab
        x = jnp.concatenate([xt_ref[j * ss:j * ss + tmr, :] for j in range(ROW_CHUNKS)], axis=1).astype(BF16)
        gu = _dot(x, wgu_ref[0]) + bgu_ref[0]
        a = jnp.minimum(gu[:, :D_FF], SWIGLU_LIMIT)
        lin = jnp.clip(gu[:, D_FF:], -SWIGLU_LIMIT, SWIGLU_LIMIT)
        act = a * (1.0 / (1.0 + jnp.exp(-SWIGLU_ALPHA * a))) * (lin + 1.0)
        y = _dot(act.astype(BF16), wdn_ref[0]) + bdn_ref[0]
        for j in range(ROW_CHUNKS):
            yt_ref[j * ss:j * ss + tmr, :] = y[:, j * LANES:(j + 1) * LANES]
        for g in range(tmr // SCATTER_BATCH):
            rows, news = [], []
            for mi in range(g * SCATTER_BATCH, (g + 1) * SCATTER_BATCH):
                tok = jnp.where(mi < rem, idx_ref[0, 0, base + mi], tb)
                wgt = wl_ref[0, 0, base + mi]
                r0 = pl.multiple_of(tok * ROW_CHUNKS, ROW_CHUNKS)
                rows.append(r0)
                news.append(out_ref[pl.ds(r0, ROW_CHUNKS), :] + wgt * yt_ref[pl.ds(mi, ROW_CHUNKS, stride=ss), :])
            for r0, new in zip(rows, news):
                out_ref[pl.ds(r0, ROW_CHUNKS), :] = new
        return carry

    lax.fori_loop(0, (cnt + tmr - 1) // tmr, tile, 0)


def _moe_super_block(t):
    for tb in (4096, 2048, 1024, 512, 256):
        if t % tb == 0:
            return tb
    raise ValueError(f"token count {t} is not a multiple of {ROW_TILE}")


def _moe(f_rows, top_i, top_w, wgu, bgu, wdn, bdn):
    t = top_i.shape[0]
    tb = _moe_super_block(t)
    n_sb = t // tb
    n_asg = tb * TOP_K
    list_len = n_asg + MOE_ROW_TILE
    ei = top_i[:, :TOP_K].reshape(n_sb, n_asg)
    wi = top_w[:, :TOP_K].reshape(n_sb, n_asg)
    order = jnp.argsort(ei, axis=1, stable=True)
    tok = (order // TOP_K).astype(jnp.int32)
    wl = jnp.take_along_axis(wi, order, axis=1)
    cnt = jnp.sum((ei[:, :, None] == jnp.arange(N_EXPERTS, dtype=jnp.int32)).astype(jnp.int32), axis=1)
    off = jnp.cumsum(cnt, axis=1) - cnt
    tok = jnp.pad(tok, ((0, 0), (0, list_len - n_asg))).reshape(n_sb, 1, list_len)
    wl = jnp.pad(wl, ((0, 0), (0, list_len - n_asg))).reshape(n_sb, 1, list_len)
    out_rows = (tb + ROW_TILE) * ROW_CHUNKS
    grid_spec = pltpu.PrefetchScalarGridSpec(
        num_scalar_prefetch=2,
        grid=(n_sb, N_EXPERTS),
        in_specs=[
            pl.BlockSpec((1, 1, list_len), lambda s, e, c, o: (s, 0, 0), memory_space=pltpu.SMEM),
            pl.BlockSpec((1, 1, list_len), lambda s, e, c, o: (s, 0, 0), memory_space=pltpu.SMEM),
            pl.BlockSpec((tb * ROW_CHUNKS, LANES), lambda s, e, c, o: (s, 0), pipeline_mode=pl.Buffered(1)),
            pl.BlockSpec((1, D_MODEL, 2 * D_FF), lambda s, e, c, o: (e, 0, 0)),
            pl.BlockSpec((1, 1, 2 * D_FF), lambda s, e, c, o: (e, 0, 0)),
            pl.BlockSpec((1, D_FF, D_MODEL), lambda s, e, c, o: (e, 0, 0)),
            pl.BlockSpec((1, 1, D_MODEL), lambda s, e, c, o: (e, 0, 0)),
        ],
        out_specs=pl.BlockSpec((out_rows, LANES), lambda s, e, c, o: (s, 0), pipeline_mode=pl.Buffered(1)),
        scratch_shapes=[pltpu.VMEM((ROW_CHUNKS * MOE_STAGE_STRIDE, LANES), F32)] * 2,
    )
    return pl.pallas_call(
        functools.partial(_moe_kernel, tb=tb),
        grid_spec=grid_spec,
        out_shape=jax.ShapeDtypeStruct((n_sb * out_rows, LANES), F32),
        compiler_params=_vmem_limit(60),
    )(cnt.reshape(-1), off.reshape(-1), tok, wl, f_rows, wgu, bgu, wdn, bdn), tb


def _post_kernel(x_ref, y_ref, mod_ref, g_ref, o_ref):
    tm = x_ref.shape[0]
    mod = mod_ref[0]
    y = jnp.concatenate([y_ref[pl.ds(j, tm, stride=ROW_CHUNKS), :] for j in range(ROW_CHUNKS)], axis=1)
    o_ref[...] = x_ref[...] + mod[:, 5 * D_MODEL:6 * D_MODEL] * (_rms(y) * g_ref[...])


def _post(x, y_rows, tb, mods3, mod_row_fn, g, seq):
    t = x.shape[0]
    tm = ROW_TILE
    tiles_per_seq = seq // tm
    tiles_per_sb = tb // tm
    return pl.pallas_call(
        _post_kernel,
        grid=(t // tm,),
        in_specs=[
            pl.BlockSpec((tm, D_MODEL), lambda i: (i, 0)),
            pl.BlockSpec((tm * ROW_CHUNKS, LANES), lambda i: ((i // tiles_per_sb) * (tiles_per_sb + 1) + i % tiles_per_sb, 0)),
            pl.BlockSpec((1, 1, N_MOD * D_MODEL), lambda i: (mod_row_fn(i // tiles_per_seq), 0, 0)),
            pl.BlockSpec((1, D_MODEL), lambda i: (0, 0)),
        ],
        out_specs=pl.BlockSpec((tm, D_MODEL), lambda i: (i, 0)),
        out_shape=jax.ShapeDtypeStruct((t, D_MODEL), F32),
        compiler_params=_vmem_limit(48),
    )(x, y_rows, mods3, g)


def _rope_tables(seq):
    lane = jnp.arange(LANES)
    d = lane % HEAD_DIM
    axis = d // (2 * ROT_FREQS)
    freq = d % ROT_FREQS
    first_half = (d % (2 * ROT_FREQS)) < ROT_FREQS
    inv_freq = ROPE_THETA ** (-jnp.arange(ROT_FREQS, dtype=F32) / ROT_FREQS)
    tpos = jnp.arange(seq)
    pos = jnp.where(axis[None, :] == 0, (tpos // GRID_W)[:, None], (tpos % GRID_W)[:, None]).astype(F32)
    ang = pos * inv_freq[freq][None, :]
    return jnp.cos(ang), jnp.sin(ang) * jnp.where(first_half, -1.0, 1.0)[None, :].astype(F32)


def _dft_tables(n):
    k = (jnp.arange(n, dtype=jnp.int32)[:, None] * jnp.arange(n, dtype=jnp.int32)[None, :]) % n
    ang = k.astype(F32) * (2.0 * math.pi / n)
    return jnp.cos(ang), jnp.sin(ang)


def kernel(x, c, ctx, c_ctx, w_mod, b_mod, g_pre_mix, g_post_mix, g_pre_ffn, g_post_ffn, w_in, g_q, g_k, conv_w,
           g_branch, w_out, w_router, b_router, w_gate_up, b_gate_up, w_down, b_down):
    batch, seq, _ = x.shape
    ctx_len = ctx.shape[1]
    depth = w_mod.shape[0]
    assert seq % ROW_TILE == 0 and ctx_len % ROW_TILE == 0 and batch + 1 <= MOD_ROWS_PAD

    cc = jnp.zeros((MOD_ROWS_PAD, D_MODEL), F32).at[:batch].set(c).at[batch].set(c_ctx)
    mods = _modulation(cc, w_mod, b_mod)
    mods3 = mods.reshape(depth * MOD_ROWS_PAD, 1, N_MOD * D_MODEL)

    cos_l, sin_l = _rope_tables(seq)
    cos_c, sin_c = jnp.ones((ctx_len, LANES), F32), jnp.zeros((ctx_len, LANES), F32)
    cn_l, sn_l = (a.astype(BF16) for a in _dft_tables(seq))
    cn_c, sn_c = (a.astype(BF16) for a in _dft_tables(ctx_len))
    c64, s64 = _dft_tables(FFT_GROUP_DIM)
    eye = jnp.eye(FFT_GROUPS, dtype=F32)
    bdc = jnp.kron(eye, c64).astype(BF16)
    bds = jnp.kron(eye, s64).astype(BF16)

    x_lat = x.reshape(batch * seq, D_MODEL)
    x_ctx = ctx.reshape(batch * ctx_len, D_MODEL)
    for layer in range(depth):
        last = layer == depth - 1
        lat_row = lambda b, layer=layer: layer * MOD_ROWS_PAD + b
        ctx_row = lambda b, layer=layer: layer * MOD_ROWS_PAD + batch
        w_in_l = w_in[layer].astype(BF16)
        gpre = g_pre_mix[layer].reshape(1, D_MODEL)
        gq2 = jnp.tile(g_q[layer], 2).reshape(1, LANES)
        gk2 = jnp.tile(g_k[layer], 2).reshape(1, LANES)
        gbr = g_branch[layer].reshape(1, MIX_W)
        wout = w_out[layer].astype(BF16)
        gpost = g_post_mix[layer].reshape(1, D_MODEL)
        gffn = g_pre_ffn[layer].reshape(1, D_MODEL)
        gpf = g_post_ffn[layer].reshape(1, D_MODEL)
        wr = jnp.zeros((D_MODEL, LANES), F32).at[:, :N_EXPERTS].set(w_router[layer])
        wrh = wr.astype(BF16)
        wrl = (wr - wrh.astype(F32)).astype(BF16)
        br = jnp.zeros((1, LANES), F32).at[0, :N_EXPERTS].set(b_router[layer])
        wgu = w_gate_up[layer].astype(BF16)
        bgu = b_gate_up[layer].reshape(N_EXPERTS, 1, 2 * D_FF)
        wdn = w_down[layer].astype(BF16)
        bdn = b_down[layer].reshape(N_EXPERTS, 1, D_MODEL)

        q, k_lat, v_lat, gb, u, fcs = _inproj(x_lat, mods3, lat_row, gpre, w_in_l, cos_l, sin_l, gq2, gk2,
                                              bdc, bds, seq, kv_only=False)
        if last:
            k_ctx, v_ctx = _inproj(x_ctx, mods3, ctx_row, gpre, w_in_l[:, K_OFF:CB_OFF], cos_c, sin_c, gq2, gk2,
                                   bdc, bds, ctx_len, kv_only=True)
        else:
            q_c, k_ctx, v_ctx, gb_c, u_c, fcs_c = _inproj(x_ctx, mods3, ctx_row, gpre, w_in_l, cos_c, sin_c,
                                                          gq2, gk2, bdc, bds, ctx_len, kv_only=False)
        attn = _attention(q, k_lat, v_lat, seq, k_ctx, v_ctx, ctx_len)
        four = _position_dft(cn_l, sn_l, fcs, seq)
        x_lat, f_rows, ti, tw = _merge(attn, gb, u, four, x_lat, mods3, lat_row, conv_w[layer], gbr, wout, gpost,
                                       gffn, wrh, wrl, br, seq)
        y_rows, tb = _moe(f_rows, ti, tw, wgu, bgu, wdn, bdn)
        x_lat = _post(x_lat, y_rows, tb, mods3, lat_row, gpf, seq)
        if not last:
            attn_c = _attention(q_c, k_ctx, v_ctx, ctx_len)
            four_c = _position_dft(cn_c, sn_c, fcs_c, ctx_len)
            x_ctx, f_rows_c, ti_c, tw_c = _merge(attn_c, gb_c, u_c, four_c, x_ctx, mods3, ctx_row, conv_w[layer],
                                                 gbr, wout, gpost, gffn, wrh, wrl, br, ctx_len)
            y_rows_c, tb_c = _moe(f_rows_c, ti_c, tw_c, wgu, bgu, wdn, bdn)
            x_ctx = _post(x_ctx, y_rows_c, tb_c, mods3, ctx_row, gpf, ctx_len)
    return x_lat.reshape(batch, seq, D_MODEL)
```

```python
import functools
import math

import jax
import jax.numpy as jnp
from jax import lax
from jax.experimental import pallas as pl
from jax.experimental.pallas import tpu as pltpu

D_MODEL = 1024
GRID_W = 64
HEAD_DIM = 64
N_Q_HEADS = 8
N_KV_HEADS = 2
Q_PER_KV = N_Q_HEADS // N_KV_HEADS
ATTN_W = N_Q_HEADS * HEAD_DIM
KV_W = N_KV_HEADS * HEAD_DIM
ROT_FREQS = HEAD_DIM // 4
ROPE_THETA = 10000.0
CONV_W = D_MODEL // 4
CONV_K = 3
FFT_W = D_MODEL // 4
FFT_GROUPS = 4
FFT_GROUP_DIM = FFT_W // FFT_GROUPS
MIX_W = ATTN_W + CONV_W + FFT_W
Q_OFF = 0
K_OFF = Q_OFF + ATTN_W
V_OFF = K_OFF + KV_W
CB_OFF = V_OFF + KV_W
CC_OFF = CB_OFF + CONV_W
CV_OFF = CC_OFF + CONV_W
F_OFF = CV_OFF + CONV_W
IN_W = F_OFF + FFT_W
N_EXPERTS = 32
TOP_K = 4
D_FF = D_MODEL
SWIGLU_LIMIT = 7.0
SWIGLU_ALPHA = 1.702
N_MOD = 6
EPS = 1e-6

LANES = 128
SUBLANES = 8
ROW_CHUNKS = D_MODEL // LANES
MOD_ROWS_PAD = 24
HALO_ROWS = 16

F32 = jnp.float32
BF16 = jnp.bfloat16
HIGHEST = lax.Precision.HIGHEST

ROW_TILE = 256
MOE_ROW_TILE = 256
MOE_TAIL_TILE = 128
MOE_STAGE_STRIDE = MOE_ROW_TILE + SUBLANES
SCATTER_BATCH = 4


def _vmem_limit(mib):
    return pltpu.CompilerParams(vmem_limit_bytes=mib * 1024 * 1024)


def _dot(a, b):
    return jnp.dot(a, b, preferred_element_type=F32)


def _mod_kernel(cc_ref, w_ref, b_ref, o_ref):
    cc = cc_ref[...]
    s = cc * (1.0 / (1.0 + jnp.exp(-cc)))
    o_ref[0] = jnp.dot(s, w_ref[0], preferred_element_type=F32, precision=HIGHEST) + b_ref[0]


def _modulation(cc, w_mod, b_mod):
    depth = w_mod.shape[0]
    tn = 1024
    return pl.pallas_call(
        _mod_kernel,
        grid=(depth, N_MOD * D_MODEL // tn),
        in_specs=[
            pl.BlockSpec((MOD_ROWS_PAD, D_MODEL), lambda l, j: (0, 0)),
            pl.BlockSpec((1, D_MODEL, tn), lambda l, j: (l, 0, j)),
            pl.BlockSpec((1, 1, tn), lambda l, j: (l, 0, j)),
        ],
        out_specs=pl.BlockSpec((1, MOD_ROWS_PAD, tn), lambda l, j: (l, 0, j)),
        out_shape=jax.ShapeDtypeStruct((depth, MOD_ROWS_PAD, N_MOD * D_MODEL), F32),
        compiler_params=_vmem_limit(32),
    )(cc, w_mod, b_mod.reshape(depth, 1, N_MOD * D_MODEL))


def _modulated_norm(x, gain, shift, scale):
    ms = jnp.mean(x * x, axis=-1, keepdims=True)
    return x * lax.rsqrt(ms + EPS) * gain * (1.0 + scale) + shift


def _head_pair_norm_rope(xc, gain, cos, sin):
    lane = lax.broadcasted_iota(jnp.int32, xc.shape, 1)
    lo_head = lane < HEAD_DIM
    x2 = xc * xc
    s_all = jnp.sum(x2, axis=-1, keepdims=True)
    s_lo = jnp.sum(jnp.where(lo_head, x2, 0.0), axis=-1, keepdims=True)
    s_hi = s_all - s_lo
    inv = jnp.where(lo_head, lax.rsqrt(s_lo * (1.0 / HEAD_DIM) + EPS), lax.rsqrt(s_hi * (1.0 / HEAD_DIM) + EPS))
    y = xc * inv * gain
    first_half = (lane % (2 * ROT_FREQS)) < ROT_FREQS
    partner = jnp.where(first_half, pltpu.roll(y, LANES - ROT_FREQS, axis=1), pltpu.roll(y, ROT_FREQS, axis=1))
    return y * cos + partner * sin


def _inproj_kernel(x_ref, mod_ref, gpre_ref, w_ref, cos_ref, sin_ref, gq_ref, gk_ref, bdc_ref, bds_ref,
                   *out_refs, kv_only):
    mod = mod_ref[0]
    h = _modulated_norm(x_ref[...], gpre_ref[...], mod[:, 0:D_MODEL], mod[:, D_MODEL:2 * D_MODEL])
    p = _dot(h.astype(BF16), w_ref[...])
    cos = cos_ref[...]
    sin = sin_ref[...]
    if kv_only:
        k_ref, v_ref = out_refs
        k_off, v_off = 0, KV_W
    else:
        q_ref, k_ref, v_ref, gb_ref, u_ref, fcs_ref = out_refs
        k_off, v_off = K_OFF, V_OFF
        for c in range(ATTN_W // LANES):
            qc = _head_pair_norm_rope(p[:, c * LANES:(c + 1) * LANES], gq_ref[...], cos, sin)
            q_ref[:, c * LANES:(c + 1) * LANES] = (qc * (HEAD_DIM ** -0.5)).astype(BF16)
        gb_ref[...] = p[:, CB_OFF:CC_OFF].astype(BF16)
        u_ref[...] = (p[:, CC_OFF:CV_OFF] * p[:, CV_OFF:F_OFF]).astype(BF16)
        f = p[:, F_OFF:IN_W].astype(BF16)
        fcs_ref[:, 0:FFT_W] = _dot(f, bdc_ref[...]).astype(BF16)
        fcs_ref[:, FFT_W:2 * FFT_W] = _dot(f, bds_ref[...]).astype(BF16)
    k_ref[...] = _head_pair_norm_rope(p[:, k_off:k_off + KV_W], gk_ref[...], cos, sin).astype(BF16)
    v_ref[...] = p[:, v_off:v_off + KV_W].astype(BF16)


def _inproj(x, mods3, mod_row_fn, gpre, w, cos, sin, gq2, gk2, bdc, bds, seq, kv_only):
    t = x.shape[0]
    tm = ROW_TILE
    tiles_per_seq = seq // tm
    width = w.shape[1]
    row = lambda i: (i, 0)
    const = lambda i: (0, 0)
    in_specs = [
        pl.BlockSpec((tm, D_MODEL), row),
        pl.BlockSpec((1, 1, N_MOD * D_MODEL), lambda i: (mod_row_fn(i // tiles_per_seq), 0, 0)),
        pl.BlockSpec((1, D_MODEL), const),
        pl.BlockSpec((D_MODEL, width), const),
        pl.BlockSpec((tm, LANES), lambda i: (i % tiles_per_seq, 0)),
        pl.BlockSpec((tm, LANES), lambda i: (i % tiles_per_seq, 0)),
        pl.BlockSpec((1, LANES), const),
        pl.BlockSpec((1, LANES), const),
        pl.BlockSpec((FFT_W, FFT_W), const),
        pl.BlockSpec((FFT_W, FFT_W), const),
    ]
    kv_shapes = [jax.ShapeDtypeStruct((t, KV_W), BF16)] * 2
    kv_specs = [pl.BlockSpec((tm, KV_W), row)] * 2
    if kv_only:
        out_shape, out_specs = kv_shapes, kv_specs
    else:
        out_shape = ([jax.ShapeDtypeStruct((t, ATTN_W), BF16)] + kv_shapes
                     + [jax.ShapeDtypeStruct((t, CONV_W), BF16)] * 2 + [jax.ShapeDtypeStruct((t, 2 * FFT_W), BF16)])
        out_specs = ([pl.BlockSpec((tm, ATTN_W), row)] + kv_specs
                     + [pl.BlockSpec((tm, CONV_W), row)] * 2 + [pl.BlockSpec((tm, 2 * FFT_W), row)])
    return pl.pallas_call(
        functools.partial(_inproj_kernel, kv_only=kv_only),
        grid=(t // tm,),
        in_specs=in_specs,
        out_specs=out_specs,
        out_shape=out_shape,
        compiler_params=_vmem_limit(48),
    )(x, mods3, gpre, w, cos, sin, gq2, gk2, bdc, bds)


def _attn_kernel(q_ref, k_ref, v_ref, *rest, has_ctx):
    if has_ctx:
        kc_ref, vc_ref, o_ref = rest
    else:
        (o_ref,) = rest
    nt_dims = (((1,), (1,)), ((), ()))
    for h in range(N_Q_HEADS):
        kv = h // Q_PER_KV
        qh = q_ref[:, h * HEAD_DIM:(h + 1) * HEAD_DIM]
        kh = k_ref[:, kv * HEAD_DIM:(kv + 1) * HEAD_DIM]
        vh = v_ref[:, kv * HEAD_DIM:(kv + 1) * HEAD_DIM]
        s = lax.dot_general(qh, kh, nt_dims, preferred_element_type=F32)
        m = jnp.max(s, axis=-1, keepdims=True)
        if has_ctx:
            kch = kc_ref[:, kv * HEAD_DIM:(kv + 1) * HEAD_DIM]
            vch = vc_ref[:, kv * HEAD_DIM:(kv + 1) * HEAD_DIM]
            s2 = lax.dot_general(qh, kch, nt_dims, preferred_element_type=F32)
            m = jnp.maximum(m, jnp.max(s2, axis=-1, keepdims=True))
        p = jnp.exp(s - m)
        l = jnp.sum(p, axis=-1, keepdims=True)
        o = _dot(p.astype(BF16), vh)
        if has_ctx:
            p2 = jnp.exp(s2 - m)
            l = l + jnp.sum(p2, axis=-1, keepdims=True)
            o = o + _dot(p2.astype(BF16), vch)
        o_ref[:, h * HEAD_DIM:(h + 1) * HEAD_DIM] = (o / l).astype(BF16)


def _attention(q, k, v, seq, kc=None, vc=None, ctx_len=None):
    t = q.shape[0]
    tq = ROW_TILE
    tiles_per_seq = seq // tq
    has_ctx = kc is not None
    in_specs = [
        pl.BlockSpec((tq, ATTN_W), lambda b, i: (b * tiles_per_seq + i, 0)),
        pl.BlockSpec((seq, KV_W), lambda b, i: (b, 0)),
        pl.BlockSpec((seq, KV_W), lambda b, i: (b, 0)),
    ]
    args = [q, k, v]
    if has_ctx:
        in_specs += [pl.BlockSpec((ctx_len, KV_W), lambda b, i: (b, 0))] * 2
        args += [kc, vc]
    return pl.pallas_call(
        functools.partial(_attn_kernel, has_ctx=has_ctx),
        grid=(t // seq, tiles_per_seq),
        in_specs=in_specs,
        out_specs=pl.BlockSpec((tq, ATTN_W), lambda b, i: (b * tiles_per_seq + i, 0)),
        out_shape=jax.ShapeDtypeStruct((t, ATTN_W), BF16),
        compiler_params=_vmem_limit(48),
    )(*args)


def _dft_kernel(cn_ref, sn_ref, fcs_ref, o_ref):
    y = _dot(cn_ref[...], fcs_ref[:, 0:FFT_W]) - _dot(sn_ref[...], fcs_ref[:, FFT_W:2 * FFT_W])
    o_ref[...] = y.astype(BF16)


def _position_dft(cn, sn, fcs, seq):
    t = fcs.shape[0]
    tn = min(seq, 1024)
    tiles_per_seq = seq // tn
    return pl.pallas_call(
        _dft_kernel,
        grid=(tiles_per_seq, t // seq),
        in_specs=[
            pl.BlockSpec((tn, seq), lambda i, b: (i, 0)),
            pl.BlockSpec((tn, seq), lambda i, b: (i, 0)),
            pl.BlockSpec((seq, 2 * FFT_W), lambda i, b: (b, 0)),
        ],
        out_specs=pl.BlockSpec((tn, FFT_W), lambda i, b: (b * tiles_per_seq + i, 0)),
        out_shape=jax.ShapeDtypeStruct((t, FFT_W), BF16),
        compiler_params=_vmem_limit(48),
    )(cn, sn, fcs)


def _rms(x):
    return x * lax.rsqrt(jnp.mean(x * x, axis=-1, keepdims=True) + EPS)


def _split_bf16(x):
    hi = x.astype(BF16)
    return hi, (x - hi.astype(F32)).astype(BF16)


def _merge_kernel(o_ref, gb_ref, u_ref, up_ref, un_ref, four_ref, x_ref, mod_ref, cw_ref, gbr_ref, wout_ref,
                  gpost_ref, gffn_ref, wrh_ref, wrl_ref, br_ref,
                  xn_ref, f_ref, ti_ref, tw_ref, *, tiles_per_seq):
    i = pl.program_id(0)
    tm = x_ref.shape[0]
    mod = mod_ref[0]
    u = u_ref[...].astype(F32)
    rowi = lax.broadcasted_iota(jnp.int32, u.shape, 0)
    first = (i % tiles_per_seq) == 0
    last = (i % tiles_per_seq) == tiles_per_seq - 1
    prev_row = jnp.where(first, 0.0, up_ref[HALO_ROWS - 1:HALO_ROWS, :].astype(F32))
    next_row = jnp.where(last, 0.0, un_ref[0:1, :].astype(F32))
    u_up = jnp.where(rowi == 0, prev_row, pltpu.roll(u, 1, axis=0))
    u_dn = jnp.where(rowi == tm - 1, next_row, pltpu.roll(u, tm - 1, axis=0))
    conv = gb_ref[...].astype(F32) * (u_up * cw_ref[0:1, :] + u * cw_ref[1:2, :] + u_dn * cw_ref[2:3, :])
    merged = jnp.concatenate(
        [_rms(o_ref[...].astype(F32)), _rms(conv), _rms(four_ref[...].astype(F32))], axis=-1) * gbr_ref[...]
    mix = _dot(merged.astype(BF16), wout_ref[...])
    xn = x_ref[...] + mod[:, 2 * D_MODEL:3 * D_MODEL] * (_rms(mix) * gpost_ref[...])
    xn_ref[...] = xn
    f = _modulated_norm(xn, gffn_ref[...], mod[:, 3 * D_MODEL:4 * D_MODEL], mod[:, 4 * D_MODEL:5 * D_MODEL])
    for j in range(ROW_CHUNKS):
        f_ref[pl.ds(j, tm, stride=ROW_CHUNKS), :] = f[:, j * LANES:(j + 1) * LANES]
    f_hi, f_lo = _split_bf16(f)
    logits = _dot(f_hi, wrh_ref[...]) + _dot(f_lo, wrh_ref[...]) + _dot(f_hi, wrl_ref[...]) + br_ref[...]
    lane = lax.broadcasted_iota(jnp.int32, logits.shape, 1)
    neg = jnp.float32(-jnp.inf)
    work = jnp.where(lane < N_EXPERTS, logits, neg)
    top_v, top_i = [], []
    for _ in range(TOP_K):
        m = jnp.max(work, axis=-1, keepdims=True)
        idx = jnp.min(jnp.where(work == m, lane, LANES), axis=-1, keepdims=True)
        top_v.append(m)
        top_i.append(idx)
        work = jnp.where(lane == idx, neg, work)
    ex = [jnp.exp(v - top_v[0]) for v in top_v]
    den = ex[0] + ex[1] + ex[2] + ex[3]
    ti = jnp.zeros(logits.shape, jnp.int32)
    tw = jnp.zeros(logits.shape, F32)
    for kk in range(TOP_K):
        ti = jnp.where(lane == kk, top_i[kk], ti)
        tw = jnp.where(lane == kk, ex[kk] / den, tw)
    ti_ref[...] = ti
    tw_ref[...] = tw


def _merge(o, gb, u, four, x, mods3, mod_row_fn, conv_w, gbr, wout, gpost, gffn, wrh, wrl, br, seq):
    t = x.shape[0]
    tm = ROW_TILE
    tiles_per_seq = seq // tm
    sub_per_tile = tm // HALO_ROWS
    n_sub = t // HALO_ROWS
    row = lambda i: (i, 0)
    const = lambda i: (0, 0)
    in_specs = [
        pl.BlockSpec((tm, ATTN_W), row),
        pl.BlockSpec((tm, CONV_W), row),
        pl.BlockSpec((tm, CONV_W), row),
        pl.BlockSpec((HALO_ROWS, CONV_W), lambda i: (jnp.maximum(i * sub_per_tile - 1, 0), 0)),
        pl.BlockSpec((HALO_ROWS, CONV_W), lambda i: (jnp.minimum((i + 1) * sub_per_tile, n_sub - 1), 0)),
        pl.BlockSpec((tm, FFT_W), row),
        pl.BlockSpec((tm, D_MODEL), row),
        pl.BlockSpec((1, 1, N_MOD * D_MODEL), lambda i: (mod_row_fn(i // tiles_per_seq), 0, 0)),
        pl.BlockSpec((CONV_K, CONV_W), const),
        pl.BlockSpec((1, MIX_W), const),
        pl.BlockSpec((MIX_W, D_MODEL), const),
        pl.BlockSpec((1, D_MODEL), const),
        pl.BlockSpec((1, D_MODEL), const),
        pl.BlockSpec((D_MODEL, LANES), const),
        pl.BlockSpec((D_MODEL, LANES), const),
        pl.BlockSpec((1, LANES), const),
    ]
    out_shape = [
        jax.ShapeDtypeStruct((t, D_MODEL), F32),
        jax.ShapeDtypeStruct((t * ROW_CHUNKS, LANES), F32),
        jax.ShapeDtypeStruct((t, LANES), jnp.int32),
        jax.ShapeDtypeStruct((t, LANES), F32),
    ]
    out_specs = [
        pl.BlockSpec((tm, D_MODEL), row),
        pl.BlockSpec((tm * ROW_CHUNKS, LANES), row),
        pl.BlockSpec((tm, LANES), row),
        pl.BlockSpec((tm, LANES), row),
    ]
    return pl.pallas_call(
        functools.partial(_merge_kernel, tiles_per_seq=tiles_per_seq),
        grid=(t // tm,),
        in_specs=in_specs,
        out_specs=out_specs,
        out_shape=out_shape,
        compiler_params=_vmem_limit(48),
    )(o, gb, u, u, u, four, x, mods3, conv_w, gbr, wout, gpost, gffn, wrh, wrl, br)


def _moe_kernel(cnt_ref, off_ref, idx_ref, wl_ref, src_ref, wgu_ref, bgu_ref, wdn_ref, bdn_ref, out_ref,
                xt_ref, yt_ref, pend_ref, *, tb):
    s = pl.program_id(0)
    e = pl.program_id(1)
    tmr = MOE_ROW_TILE
    ss = MOE_STAGE_STRIDE

    @pl.when(e == 0)
    def _():
        out_ref[...] = jnp.zeros(out_ref.shape, F32)
        yt_ref[...] = jnp.zeros(yt_ref.shape, F32)
        pend_ref[0] = 0
        pend_ref[1] = 0

    cnt = cnt_ref[s * N_EXPERTS + e]
    off = off_ref[s * N_EXPERTS + e]

    def scatter_pending():
        base = pend_ref[0]
        rem = pend_ref[1]
        for g in range(tmr // SCATTER_BATCH):
            rows, news = [], []
            for mi in range(g * SCATTER_BATCH, (g + 1) * SCATTER_BATCH):
                tok = jnp.where(mi < rem, idx_ref[0, 0, base + mi], tb)
                wgt = wl_ref[0, 0, base + mi]
                r0 = pl.multiple_of(tok * ROW_CHUNKS, ROW_CHUNKS)
                rows.append(r0)
                news.append(out_ref[pl.ds(r0, ROW_CHUNKS), :] + wgt * yt_ref[pl.ds(mi, ROW_CHUNKS, stride=ss), :])
            for r0, new in zip(rows, news):
                out_ref[pl.ds(r0, ROW_CHUNKS), :] = new

    def expert_tile(base, rem, rows):
        scatter_pending()
        for mi in range(rows):
            tok = idx_ref[0, 0, base + mi]
            slab = src_ref[pl.ds(pl.multiple_of(tok * ROW_CHUNKS, ROW_CHUNKS), ROW_CHUNKS), :]
            xt_ref[pl.ds(mi, ROW_CHUNKS, stride=ss), :] = slab
        x = jnp.concatenate([xt_ref[j * ss:j * ss + rows, :] for j in range(ROW_CHUNKS)], axis=1).astype(BF16)
        gu = _dot(x, wgu_ref[0, 0]) + bgu_ref[0, 0]
        a = jnp.minimum(gu[:, :D_FF], SWIGLU_LIMIT)
        lin = jnp.clip(gu[:, D_FF:], -SWIGLU_LIMIT, SWIGLU_LIMIT)
        act = a * (1.0 / (1.0 + jnp.exp(-SWIGLU_ALPHA * a))) * (lin + 1.0)
        y = _dot(act.astype(BF16), wdn_ref[0, 0]) + bdn_ref[0, 0]
        for j in range(ROW_CHUNKS):
            yt_ref[j * ss:j * ss + rows, :] = y[:, j * LANES:(j + 1) * LANES]
        pend_ref[0] = base
        pend_ref[1] = jnp.minimum(rem, rows)

    tail = cnt % tmr
    n_full = cnt // tmr + (tail > MOE_TAIL_TILE).astype(jnp.int32)

    def full_tile(t, carry):
        expert_tile(off + t * tmr, cnt - t * tmr, tmr)
        return carry

    lax.fori_loop(0, n_full, full_tile, 0)

    @pl.when((tail > 0) & (tail <= MOE_TAIL_TILE))
    def _():
        expert_tile(off + n_full * tmr, tail, MOE_TAIL_TILE)

    @pl.when(e == N_EXPERTS - 1)
    def _():
        scatter_pending()
        pend_ref[1] = 0


def _moe_super_block(t):
    for tb in (4096, 2048, 1024, 512, 256):
        if t % tb == 0:
            return tb
    raise ValueError(f"token count {t} is not a multiple of {ROW_TILE}")


def _moe(f_rows, top_i, top_w, wgu, bgu, wdn, bdn, layer):
    t = top_i.shape[0]
    tb = _moe_super_block(t)
    n_sb = t // tb
    n_asg = tb * TOP_K
    list_len = n_asg + MOE_ROW_TILE
    ei = top_i[:, :TOP_K].reshape(n_sb, n_asg)
    wi = top_w[:, :TOP_K].reshape(n_sb, n_asg)
    order = jnp.argsort(ei, axis=1, stable=True)
    tok = (order // TOP_K).astype(jnp.int32)
    wl = jnp.take_along_axis(wi, order, axis=1)
    cnt = jnp.sum((ei[:, :, None] == jnp.arange(N_EXPERTS, dtype=jnp.int32)).astype(jnp.int32), axis=1)
    off = jnp.cumsum(cnt, axis=1) - cnt
    tok = jnp.pad(tok, ((0, 0), (0, list_len - n_asg))).reshape(n_sb, 1, list_len)
    wl = jnp.pad(wl, ((0, 0), (0, list_len - n_asg))).reshape(n_sb, 1, list_len)
    out_rows = (tb + ROW_TILE) * ROW_CHUNKS
    grid_spec = pltpu.PrefetchScalarGridSpec(
        num_scalar_prefetch=2,
        grid=(n_sb, N_EXPERTS),
        in_specs=[
            pl.BlockSpec((1, 1, list_len), lambda s, e, c, o: (s, 0, 0), memory_space=pltpu.SMEM),
            pl.BlockSpec((1, 1, list_len), lambda s, e, c, o: (s, 0, 0), memory_space=pltpu.SMEM),
            pl.BlockSpec((tb * ROW_CHUNKS, LANES), lambda s, e, c, o: (s, 0), pipeline_mode=pl.Buffered(1)),
            pl.BlockSpec((1, 1, D_MODEL, 2 * D_FF), lambda s, e, c, o: (layer, e, 0, 0)),
            pl.BlockSpec((1, 1, 1, 2 * D_FF), lambda s, e, c, o: (layer, e, 0, 0)),
            pl.BlockSpec((1, 1, D_FF, D_MODEL), lambda s, e, c, o: (layer, e, 0, 0)),
            pl.BlockSpec((1, 1, 1, D_MODEL), lambda s, e, c, o: (layer, e, 0, 0)),
        ],
        out_specs=pl.BlockSpec((out_rows, LANES), lambda s, e, c, o: (s, 0), pipeline_mode=pl.Buffered(1)),
        scratch_shapes=[pltpu.VMEM((ROW_CHUNKS * MOE_STAGE_STRIDE, LANES), F32)] * 2 + [pltpu.SMEM((2,), jnp.int32)],
    )
    return pl.pallas_call(
        functools.partial(_moe_kernel, tb=tb),
        grid_spec=grid_spec,
        out_shape=jax.ShapeDtypeStruct((n_sb * out_rows, LANES), F32),
        compiler_params=_vmem_limit(60),
    )(cnt.reshape(-1), off.reshape(-1), tok, wl, f_rows, wgu, bgu, wdn, bdn), tb


def _post_kernel(x_ref, y_ref, mod_ref, g_ref, o_ref):
    tm = x_ref.shape[0]
    mod = mod_ref[0]
    y = jnp.concatenate([y_ref[pl.ds(j, tm, stride=ROW_CHUNKS), :] for j in range(ROW_CHUNKS)], axis=1)
    o_ref[...] = x_ref[...] + mod[:, 5 * D_MODEL:6 * D_MODEL] * (_rms(y) * g_ref[...])


def _post(x, y_rows, tb, mods3, mod_row_fn, g, seq):
    t = x.shape[0]
    tm = ROW_TILE
    tiles_per_seq = seq // tm
    tiles_per_sb = tb // tm
    return pl.pallas_call(
        _post_kernel,
        grid=(t // tm,),
        in_specs=[
            pl.BlockSpec((tm, D_MODEL), lambda i: (i, 0)),
            pl.BlockSpec((tm * ROW_CHUNKS, LANES), lambda i: ((i // tiles_per_sb) * (tiles_per_sb + 1) + i % tiles_per_sb, 0)),
            pl.BlockSpec((1, 1, N_MOD * D_MODEL), lambda i: (mod_row_fn(i // tiles_per_seq), 0, 0)),
            pl.BlockSpec((1, D_MODEL), lambda i: (0, 0)),
        ],
        out_specs=pl.BlockSpec((tm, D_MODEL), lambda i: (i, 0)),
        out_shape=jax.ShapeDtypeStruct((t, D_MODEL), F32),
        compiler_params=_vmem_limit(48),
    )(x, y_rows, mods3, g)


def _rope_tables(seq):
    lane = jnp.arange(LANES)
    d = lane % HEAD_DIM
    axis = d // (2 * ROT_FREQS)
    freq = d % ROT_FREQS
    first_half = (d % (2 * ROT_FREQS)) < ROT_FREQS
    inv_freq = ROPE_THETA ** (-jnp.arange(ROT_FREQS, dtype=F32) / ROT_FREQS)
    tpos = jnp.arange(seq)
    pos = jnp.where(axis[None, :] == 0, (tpos // GRID_W)[:, None], (tpos % GRID_W)[:, None]).astype(F32)
    ang = pos * inv_freq[freq][None, :]
    return jnp.cos(ang), jnp.sin(ang) * jnp.where(first_half, -1.0, 1.0)[None, :].astype(F32)


def _dft_tables(n):
    k = (jnp.arange(n, dtype=jnp.int32)[:, None] * jnp.arange(n, dtype=jnp.int32)[None, :]) % n
    ang = k.astype(F32) * (2.0 * math.pi / n)
    return jnp.cos(ang), jnp.sin(ang)


def kernel(x, c, ctx, c_ctx, w_mod, b_mod, g_pre_mix, g_post_mix, g_pre_ffn, g_post_ffn, w_in, g_q, g_k, conv_w,
           g_branch, w_out, w_router, b_router, w_gate_up, b_gate_up, w_down, b_down):
    batch, seq, _ = x.shape
    ctx_len = ctx.shape[1]
    depth = w_mod.shape[0]
    assert seq % ROW_TILE == 0 and ctx_len % ROW_TILE == 0 and batch + 1 <= MOD_ROWS_PAD

    cc = jnp.zeros((MOD_ROWS_PAD, D_MODEL), F32).at[:batch].set(c).at[batch].set(c_ctx)
    mods = _modulation(cc, w_mod, b_mod)
    mods3 = mods.reshape(depth * MOD_ROWS_PAD, 1, N_MOD * D_MODEL)

    cos_l, sin_l = _rope_tables(seq)
    cos_c, sin_c = jnp.ones((ctx_len, LANES), F32), jnp.zeros((ctx_len, LANES), F32)
    cn_l, sn_l = (a.astype(BF16) for a in _dft_tables(seq))
    cn_c, sn_c = (a.astype(BF16) for a in _dft_tables(ctx_len))
    c64, s64 = _dft_tables(FFT_GROUP_DIM)
    eye = jnp.eye(FFT_GROUPS, dtype=F32)
    bdc = jnp.kron(eye, c64).astype(BF16)
    bds = jnp.kron(eye, s64).astype(BF16)

    wgu = w_gate_up.astype(BF16)
    bgu = b_gate_up.reshape(depth, N_EXPERTS, 1, 2 * D_FF)
    wdn = w_down.astype(BF16)
    bdn = b_down.reshape(depth, N_EXPERTS, 1, D_MODEL)

    x_lat = x.reshape(batch * seq, D_MODEL)
    x_ctx = ctx.reshape(batch * ctx_len, D_MODEL)
    for layer in range(depth):
        last = layer == depth - 1
        lat_row = lambda b, layer=layer: layer * MOD_ROWS_PAD + b
        ctx_row = lambda b, layer=layer: layer * MOD_ROWS_PAD + batch
        w_in_l = w_in[layer].astype(BF16)
        gpre = g_pre_mix[layer].reshape(1, D_MODEL)
        gq2 = jnp.tile(g_q[layer], 2).reshape(1, LANES)
        gk2 = jnp.tile(g_k[layer], 2).reshape(1, LANES)
        gbr = g_branch[layer].reshape(1, MIX_W)
        wout = w_out[layer].astype(BF16)
        gpost = g_post_mix[layer].reshape(1, D_MODEL)
        gffn = g_pre_ffn[layer].reshape(1, D_MODEL)
        gpf = g_post_ffn[layer].reshape(1, D_MODEL)
        wr = jnp.zeros((D_MODEL, LANES), F32).at[:, :N_EXPERTS].set(w_router[layer])
        wrh = wr.astype(BF16)
        wrl = (wr - wrh.astype(F32)).astype(BF16)
        br = jnp.zeros((1, LANES), F32).at[0, :N_EXPERTS].set(b_router[layer])

        q, k_lat, v_lat, gb, u, fcs = _inproj(x_lat, mods3, lat_row, gpre, w_in_l, cos_l, sin_l, gq2, gk2,
                                              bdc, bds, seq, kv_only=False)
        if last:
            k_ctx, v_ctx = _inproj(x_ctx, mods3, ctx_row, gpre, w_in_l[:, K_OFF:CB_OFF], cos_c, sin_c, gq2, gk2,
                                   bdc, bds, ctx_len, kv_only=True)
        else:
            q_c, k_ctx, v_ctx, gb_c, u_c, fcs_c = _inproj(x_ctx, mods3, ctx_row, gpre, w_in_l, cos_c, sin_c,
                                                          gq2, gk2, bdc, bds, ctx_len, kv_only=False)
        attn = _attention(q, k_lat, v_lat, seq, k_ctx, v_ctx, ctx_len)
        four = _position_dft(cn_l, sn_l, fcs, seq)
        x_lat, f_rows, ti, tw = _merge(attn, gb, u, four, x_lat, mods3, lat_row, conv_w[layer], gbr, wout, gpost,
                                       gffn, wrh, wrl, br, seq)
        y_rows, tb = _moe(f_rows, ti, tw, wgu, bgu, wdn, bdn, layer)
        x_lat = _post(x_lat, y_rows, tb, mods3, lat_row, gpf, seq)
        if not last:
            attn_c = _attention(q_c, k_ctx, v_ctx, ctx_len)
            four_c = _position_dft(cn_c, sn_c, fcs_c, ctx_len)
            x_ctx, f_rows_c, ti_c, tw_c = _merge(attn_c, gb_c, u_c, four_c, x_ctx, mods3, ctx_row, conv_w[layer],
                                                 gbr, wout, gpost, gffn, wrh, wrl, br, ctx_len)
            y_rows_c, tb_c = _moe(f_rows_c, ti_c, tw_c, wgu, bgu, wdn, bdn, layer)
            x_ctx = _post(x_ctx, y_rows_c, tb_c, mods3, ctx_row, gpf, ctx_len)
    return x_lat.reshape(batch, seq, D_MODEL)
```

```python
import functools
import math

import jax
import jax.numpy as jnp
from jax import lax
from jax.experimental import pallas as pl
from jax.experimental.pallas import tpu as pltpu

D_MODEL = 1024
GRID_W = 64
HEAD_DIM = 64
N_Q_HEADS = 8
N_KV_HEADS = 2
Q_PER_KV = N_Q_HEADS // N_KV_HEADS
ATTN_W = N_Q_HEADS * HEAD_DIM
KV_W = N_KV_HEADS * HEAD_DIM
ROT_FREQS = HEAD_DIM // 4
ROPE_THETA = 10000.0
CONV_W = D_MODEL // 4
CONV_K = 3
FFT_W = D_MODEL // 4
FFT_GROUPS = 4
FFT_GROUP_DIM = FFT_W // FFT_GROUPS
MIX_W = ATTN_W + CONV_W + FFT_W
Q_OFF = 0
K_OFF = Q_OFF + ATTN_W
V_OFF = K_OFF + KV_W
CB_OFF = V_OFF + KV_W
CC_OFF = CB_OFF + CONV_W
CV_OFF = CC_OFF + CONV_W
F_OFF = CV_OFF + CONV_W
IN_W = F_OFF + FFT_W
N_EXPERTS = 32
TOP_K = 4
D_FF = D_MODEL
SWIGLU_LIMIT = 7.0
SWIGLU_ALPHA = 1.702
N_MOD = 6
EPS = 1e-6

LANES = 128
SUBLANES = 8
ROW_CHUNKS = D_MODEL // LANES
MOD_ROWS_PAD = 24
HALO_ROWS = 16

F32 = jnp.float32
BF16 = jnp.bfloat16
HIGHEST = lax.Precision.HIGHEST

Q_SCALE = (HEAD_DIM ** -0.5) * math.log2(math.e)

ROW_TILE = 256
MERGE_ROW_TILE = 512
MOE_ROW_TILE = 256
MOE_TAIL_TILE = 128
MOE_STAGE_STRIDE = MOE_ROW_TILE + SUBLANES
SCATTER_BATCH = 4


def _vmem_limit(mib):
    return pltpu.CompilerParams(vmem_limit_bytes=mib * 1024 * 1024)


def _dot(a, b):
    return jnp.dot(a, b, preferred_element_type=F32)


def _mod_kernel(cc_ref, w_ref, b_ref, o_ref):
    cc = cc_ref[...]
    s = cc * (1.0 / (1.0 + jnp.exp(-cc)))
    o_ref[0] = jnp.dot(s, w_ref[0], preferred_element_type=F32, precision=HIGHEST) + b_ref[0]


def _modulation(cc, w_mod, b_mod):
    depth = w_mod.shape[0]
    tn = 1024
    return pl.pallas_call(
        _mod_kernel,
        grid=(depth, N_MOD * D_MODEL // tn),
        in_specs=[
            pl.BlockSpec((MOD_ROWS_PAD, D_MODEL), lambda l, j: (0, 0)),
            pl.BlockSpec((1, D_MODEL, tn), lambda l, j: (l, 0, j)),
            pl.BlockSpec((1, 1, tn), lambda l, j: (l, 0, j)),
        ],
        out_specs=pl.BlockSpec((1, MOD_ROWS_PAD, tn), lambda l, j: (l, 0, j)),
        out_shape=jax.ShapeDtypeStruct((depth, MOD_ROWS_PAD, N_MOD * D_MODEL), F32),
        compiler_params=_vmem_limit(32),
    )(cc, w_mod, b_mod.reshape(depth, 1, N_MOD * D_MODEL))


def _modulated_norm(x, gain, shift, scale):
    ms = jnp.mean(x * x, axis=-1, keepdims=True)
    return x * lax.rsqrt(ms + EPS) * gain * (1.0 + scale) + shift


def _head_pair_norm_rope(xc, gain, cos, sin):
    lane = lax.broadcasted_iota(jnp.int32, xc.shape, 1)
    lo_head = lane < HEAD_DIM
    x2 = xc * xc
    s_all = jnp.sum(x2, axis=-1, keepdims=True)
    s_lo = jnp.sum(jnp.where(lo_head, x2, 0.0), axis=-1, keepdims=True)
    s_hi = s_all - s_lo
    inv = jnp.where(lo_head, lax.rsqrt(s_lo * (1.0 / HEAD_DIM) + EPS), lax.rsqrt(s_hi * (1.0 / HEAD_DIM) + EPS))
    y = xc * inv * gain
    first_half = (lane % (2 * ROT_FREQS)) < ROT_FREQS
    partner = jnp.where(first_half, pltpu.roll(y, LANES - ROT_FREQS, axis=1), pltpu.roll(y, ROT_FREQS, axis=1))
    return y * cos + partner * sin


def _inproj_kernel(x_ref, mod_ref, gpre_ref, w_ref, cos_ref, sin_ref, gq_ref, gk_ref, bdc_ref, bds_ref,
                   *out_refs, kv_only):
    mod = mod_ref[0]
    h = _modulated_norm(x_ref[...], gpre_ref[...], mod[:, 0:D_MODEL], mod[:, D_MODEL:2 * D_MODEL])
    p = _dot(h.astype(BF16), w_ref[...])
    cos = cos_ref[...]
    sin = sin_ref[...]
    if kv_only:
        k_ref, vt_ref = out_refs
        k_off, v_off = 0, KV_W
    else:
        q_ref, k_ref, vt_ref, gb_ref, u_ref, fcs_ref = out_refs
        k_off, v_off = K_OFF, V_OFF
        for c in range(ATTN_W // LANES):
            qc = _head_pair_norm_rope(p[:, c * LANES:(c + 1) * LANES], gq_ref[...], cos, sin)
            q_ref[:, c * LANES:(c + 1) * LANES] = (qc * Q_SCALE).astype(BF16)
        gb_ref[...] = p[:, CB_OFF:CC_OFF].astype(BF16)
        u_ref[...] = (p[:, CC_OFF:CV_OFF] * p[:, CV_OFF:F_OFF]).astype(BF16)
        f = p[:, F_OFF:IN_W].astype(BF16)
        fcs_ref[:, 0:FFT_W] = _dot(f, bdc_ref[...]).astype(BF16)
        fcs_ref[:, FFT_W:2 * FFT_W] = _dot(f, bds_ref[...]).astype(BF16)
    kk = _head_pair_norm_rope(p[:, k_off:k_off + KV_W], gk_ref[...], cos, sin)
    lane = lax.broadcasted_iota(jnp.int32, kk.shape, 1)
    k0_lo = jnp.where(lane < HEAD_DIM, kk, 0.0)
    k1_hi = jnp.where(lane < HEAD_DIM, 0.0, kk)
    k_ref[:, 0 * LANES:1 * LANES] = k0_lo.astype(BF16)
    k_ref[:, 1 * LANES:2 * LANES] = pltpu.roll(k0_lo, HEAD_DIM, axis=1).astype(BF16)
    k_ref[:, 2 * LANES:3 * LANES] = pltpu.roll(k1_hi, HEAD_DIM, axis=1).astype(BF16)
    k_ref[:, 3 * LANES:4 * LANES] = k1_hi.astype(BF16)
    vt_ref[...] = p[:, v_off:v_off + KV_W].T.astype(BF16)


def _inproj(x, mods3, mod_row_fn, gpre, w, cos, sin, gq2, gk2, bdc, bds, seq, kv_only):
    t = x.shape[0]
    tm = min(MERGE_ROW_TILE, seq)
    tiles_per_seq = seq // tm
    width = w.shape[1]
    row = lambda i: (i, 0)
    const = lambda i: (0, 0)
    in_specs = [
        pl.BlockSpec((tm, D_MODEL), row),
        pl.BlockSpec((1, 1, N_MOD * D_MODEL), lambda i: (mod_row_fn(i // tiles_per_seq), 0, 0)),
        pl.BlockSpec((1, D_MODEL), const),
        pl.BlockSpec((D_MODEL, width), const),
        pl.BlockSpec((tm, LANES), lambda i: (i % tiles_per_seq, 0)),
        pl.BlockSpec((tm, LANES), lambda i: (i % tiles_per_seq, 0)),
        pl.BlockSpec((1, LANES), const),
        pl.BlockSpec((1, LANES), const),
        pl.BlockSpec((FFT_W, FFT_W), const),
        pl.BlockSpec((FFT_W, FFT_W), const),
    ]
    kv_shapes = [jax.ShapeDtypeStruct((t, K_VARIANTS_W), BF16), jax.ShapeDtypeStruct((KV_W, t), BF16)]
    kv_specs = [pl.BlockSpec((tm, K_VARIANTS_W), row), pl.BlockSpec((KV_W, tm), lambda i: (0, i))]
    if kv_only:
        out_shape, out_specs = kv_shapes, kv_specs
    else:
        out_shape = ([jax.ShapeDtypeStruct((t, ATTN_W), BF16)] + kv_shapes
                     + [jax.ShapeDtypeStruct((t, CONV_W), BF16)] * 2 + [jax.ShapeDtypeStruct((t, 2 * FFT_W), BF16)])
        out_specs = ([pl.BlockSpec((tm, ATTN_W), row)] + kv_specs
                     + [pl.BlockSpec((tm, CONV_W), row)] * 2 + [pl.BlockSpec((tm, 2 * FFT_W), row)])
    return pl.pallas_call(
        functools.partial(_inproj_kernel, kv_only=kv_only),
        grid=(t // tm,),
        in_specs=in_specs,
        out_specs=out_specs,
        out_shape=out_shape,
        compiler_params=_vmem_limit(48),
    )(x, mods3, gpre, w, cos, sin, gq2, gk2, bdc, bds)


K_VARIANTS_W = 2 * N_KV_HEADS * LANES
ATTN_KEY_CHUNK = 256


def _attn_kernel(q_ref, k_ref, vt_ref, *rest, has_ctx):
    if has_ctx:
        kc_ref, vct_ref, o_ref, st_ref = rest
    else:
        o_ref, st_ref = rest
    nt_dims = (((1,), (1,)), ((), ()))
    n_lat = k_ref.shape[0]
    n_keys = st_ref.shape[1]
    tq = st_ref.shape[2]
    ck = ATTN_KEY_CHUNK

    def scores(h):
        var = (h // Q_PER_KV) * 2 + h % 2
        qp = q_ref[:, (h // 2) * LANES:(h // 2 + 1) * LANES]
        n_split = 4 if n_lat % (4 * ATTN_KEY_CHUNK) == 0 else 1
        step = n_lat // n_split
        for r in range(n_split):
            st_ref[h % 2, r * step:(r + 1) * step, :] = lax.dot_general(
                k_ref[r * step:(r + 1) * step, var * LANES:(var + 1) * LANES], qp, nt_dims,
                preferred_element_type=F32)
        if has_ctx:
            st_ref[h % 2, n_lat:n_keys, :] = lax.dot_general(kc_ref[:, var * LANES:(var + 1) * LANES], qp, nt_dims,
                                                             preferred_element_type=F32)

    outs = []
    scores(0)
    for h in range(N_Q_HEADS):
        kv = h // Q_PER_KV
        lo, hi = kv * HEAD_DIM, (kv + 1) * HEAD_DIM
        if h + 1 < N_Q_HEADS:
            scores(h + 1)
        slot = h % 2
        n_chunks = n_keys // ck

        def chunk8(c):
            return st_ref[slot, c * ck:(c + 1) * ck, :].reshape(ck // SUBLANES, SUBLANES, tq)

        m8 = jnp.max(chunk8(0), axis=0)
        for c in range(1, n_chunks):
            m8 = jnp.maximum(m8, jnp.max(chunk8(c), axis=0))
        m = jnp.max(m8, axis=0, keepdims=True)
        l8 = None
        ot = None
        for c in range(n_chunks):
            p = jnp.exp2(st_ref[slot, c * ck:(c + 1) * ck, :] - m)
            lc = jnp.sum(p.reshape(ck // SUBLANES, SUBLANES, tq), axis=0)
            if c * ck < n_lat:
                vt = vt_ref[lo:hi, c * ck:(c + 1) * ck]
            else:
                vt = vct_ref[lo:hi, c * ck - n_lat:(c + 1) * ck - n_lat]
            oc = _dot(vt, p.astype(BF16))
            l8 = lc if l8 is None else l8 + lc
            ot = oc if ot is None else ot + oc
        l = jnp.sum(l8, axis=0, keepdims=True)
        outs.append(ot / l)
    o_ref[...] = jnp.concatenate(outs, axis=0).T.astype(BF16)


def _attention(q, k, vt, seq, kc=None, vct=None, ctx_len=None):
    t = q.shape[0]
    tq = ROW_TILE
    tiles_per_seq = seq // tq
    has_ctx = kc is not None
    in_specs = [
        pl.BlockSpec((tq, ATTN_W), lambda b, i: (b * tiles_per_seq + i, 0)),
        pl.BlockSpec((seq, K_VARIANTS_W), lambda b, i: (b, 0)),
        pl.BlockSpec((KV_W, seq), lambda b, i: (0, b)),
    ]
    args = [q, k, vt]
    if has_ctx:
        in_specs += [pl.BlockSpec((ctx_len, K_VARIANTS_W), lambda b, i: (b, 0)),
                     pl.BlockSpec((KV_W, ctx_len), lambda b, i: (0, b))]
        args += [kc, vct]
    return pl.pallas_call(
        functools.partial(_attn_kernel, has_ctx=has_ctx),
        grid=(t // seq, tiles_per_seq),
        in_specs=in_specs,
        out_specs=pl.BlockSpec((tq, ATTN_W), lambda b, i: (b * tiles_per_seq + i, 0)),
        out_shape=jax.ShapeDtypeStruct((t, ATTN_W), BF16),
        scratch_shapes=[pltpu.VMEM((2, seq + (ctx_len if has_ctx else 0), tq), F32)],
        compiler_params=_vmem_limit(48),
    )(*args)


def _dft_kernel(cn_ref, sn_ref, fcs_ref, o_ref):
    y = _dot(cn_ref[...], fcs_ref[:, 0:FFT_W]) - _dot(sn_ref[...], fcs_ref[:, FFT_W:2 * FFT_W])
    o_ref[...] = y.astype(BF16)


def _position_dft(cn, sn, fcs, seq):
    t = fcs.shape[0]
    tn = min(seq, 1024)
    tiles_per_seq = seq // tn
    return pl.pallas_call(
        _dft_kernel,
        grid=(tiles_per_seq, t // seq),
        in_specs=[
            pl.BlockSpec((tn, seq), lambda i, b: (i, 0)),
            pl.BlockSpec((tn, seq), lambda i, b: (i, 0)),
            pl.BlockSpec((seq, 2 * FFT_W), lambda i, b: (b, 0)),
        ],
        out_specs=pl.BlockSpec((tn, FFT_W), lambda i, b: (b * tiles_per_seq + i, 0)),
        out_shape=jax.ShapeDtypeStruct((t, FFT_W), BF16),
        compiler_params=_vmem_limit(48),
    )(cn, sn, fcs)


def _rms(x):
    return x * lax.rsqrt(jnp.mean(x * x, axis=-1, keepdims=True) + EPS)


def _split_bf16(x):
    hi = x.astype(BF16)
    return hi, (x - hi.astype(F32)).astype(BF16)


def _merge_kernel(o_ref, gb_ref, u_ref, up_ref, un_ref, four_ref, x_ref, mod_ref, cw_ref, gbr_ref, wout_ref,
                  gpost_ref, gffn_ref, wrh_ref, wrl_ref, br_ref,
                  xn_ref, f_ref, ti_ref, tw_ref, *, tiles_per_seq):
    i = pl.program_id(0)
    tm = x_ref.shape[0]
    mod = mod_ref[0]
    u = u_ref[...].astype(F32)
    rowi = lax.broadcasted_iota(jnp.int32, u.shape, 0)
    first = (i % tiles_per_seq) == 0
    last = (i % tiles_per_seq) == tiles_per_seq - 1
    prev_row = jnp.where(first, 0.0, up_ref[HALO_ROWS - 1:HALO_ROWS, :].astype(F32))
    next_row = jnp.where(last, 0.0, un_ref[0:1, :].astype(F32))
    u_up = jnp.where(rowi == 0, prev_row, pltpu.roll(u, 1, axis=0))
    u_dn = jnp.where(rowi == tm - 1, next_row, pltpu.roll(u, tm - 1, axis=0))
    conv = gb_ref[...].astype(F32) * (u_up * cw_ref[0:1, :] + u * cw_ref[1:2, :] + u_dn * cw_ref[2:3, :])
    merged = jnp.concatenate(
        [_rms(o_ref[...].astype(F32)), _rms(conv), _rms(four_ref[...].astype(F32))], axis=-1) * gbr_ref[...]
    mix = _dot(merged.astype(BF16), wout_ref[...])
    xn = x_ref[...] + mod[:, 2 * D_MODEL:3 * D_MODEL] * (_rms(mix) * gpost_ref[...])
    xn_ref[...] = xn
    f = _modulated_norm(xn, gffn_ref[...], mod[:, 3 * D_MODEL:4 * D_MODEL], mod[:, 4 * D_MODEL:5 * D_MODEL])
    for j in range(ROW_CHUNKS):
        f_ref[pl.ds(j, tm, stride=ROW_CHUNKS), :] = f[:, j * LANES:(j + 1) * LANES]
    f_hi, f_lo = _split_bf16(f)
    logits = _dot(f_hi, wrh_ref[...]) + _dot(f_lo, wrh_ref[...]) + _dot(f_hi, wrl_ref[...]) + br_ref[...]
    lane = lax.broadcasted_iota(jnp.int32, logits.shape, 1)
    neg = jnp.float32(-jnp.inf)
    work = jnp.where(lane < N_EXPERTS, logits, neg)
    top_v, top_i = [], []
    for _ in range(TOP_K):
        m = jnp.max(work, axis=-1, keepdims=True)
        idx = jnp.min(jnp.where(work == m, lane, LANES), axis=-1, keepdims=True)
        top_v.append(m)
        top_i.append(idx)
        work = jnp.where(lane == idx, neg, work)
    ex = [jnp.exp(v - top_v[0]) for v in top_v]
    den = ex[0] + ex[1] + ex[2] + ex[3]
    ti = jnp.zeros(logits.shape, jnp.int32)
    tw = jnp.zeros(logits.shape, F32)
    for kk in range(TOP_K):
        ti = jnp.where(lane == kk, top_i[kk], ti)
        tw = jnp.where(lane == kk, ex[kk] / den, tw)
    ti_ref[...] = ti
    tw_ref[...] = tw


def _merge(o, gb, u, four, x, mods3, mod_row_fn, conv_w, gbr, wout, gpost, gffn, wrh, wrl, br, seq):
    t = x.shape[0]
    tm = min(MERGE_ROW_TILE, seq)
    tiles_per_seq = seq // tm
    sub_per_tile = tm // HALO_ROWS
    n_sub = t // HALO_ROWS
    row = lambda i: (i, 0)
    const = lambda i: (0, 0)
    in_specs = [
        pl.BlockSpec((tm, ATTN_W), row),
        pl.BlockSpec((tm, CONV_W), row),
        pl.BlockSpec((tm, CONV_W), row),
        pl.BlockSpec((HALO_ROWS, CONV_W), lambda i: (jnp.maximum(i * sub_per_tile - 1, 0), 0)),
        pl.BlockSpec((HALO_ROWS, CONV_W), lambda i: (jnp.minimum((i + 1) * sub_per_tile, n_sub - 1), 0)),
        pl.BlockSpec((tm, FFT_W), row),
        pl.BlockSpec((tm, D_MODEL), row),
        pl.BlockSpec((1, 1, N_MOD * D_MODEL), lambda i: (mod_row_fn(i // tiles_per_seq), 0, 0)),
        pl.BlockSpec((CONV_K, CONV_W), const),
        pl.BlockSpec((1, MIX_W), const),
        pl.BlockSpec((MIX_W, D_MODEL), const),
        pl.BlockSpec((1, D_MODEL), const),
        pl.BlockSpec((1, D_MODEL), const),
        pl.BlockSpec((D_MODEL, LANES), const),
        pl.BlockSpec((D_MODEL, LANES), const),
        pl.BlockSpec((1, LANES), const),
    ]
    out_shape = [
        jax.ShapeDtypeStruct((t, D_MODEL), F32),
        jax.ShapeDtypeStruct((t * ROW_CHUNKS, LANES), F32),
        jax.ShapeDtypeStruct((t, LANES), jnp.int32),
        jax.ShapeDtypeStruct((t, LANES), F32),
    ]
    out_specs = [
        pl.BlockSpec((tm, D_MODEL), row),
        pl.BlockSpec((tm * ROW_CHUNKS, LANES), row),
        pl.BlockSpec((tm, LANES), row),
        pl.BlockSpec((tm, LANES), row),
    ]
    return pl.pallas_call(
        functools.partial(_merge_kernel, tiles_per_seq=tiles_per_seq),
        grid=(t // tm,),
        in_specs=in_specs,
        out_specs=out_specs,
        out_shape=out_shape,
        compiler_params=_vmem_limit(48),
    )(o, gb, u, u, u, four, x, mods3, conv_w, gbr, wout, gpost, gffn, wrh, wrl, br)


def _moe_kernel(cnt_ref, off_ref, idx_ref, wl_ref, src_ref, wgu_ref, bgu_ref, wdn_ref, bdn_ref, out_ref,
                xt_ref, yt_ref, pend_ref, *, tb):
    s = pl.program_id(0)
    e = pl.program_id(1)
    tmr = MOE_ROW_TILE
    ss = MOE_STAGE_STRIDE

    @pl.when(e == 0)
    def _():
        out_ref[...] = jnp.zeros(out_ref.shape, F32)
        yt_ref[...] = jnp.zeros(yt_ref.shape, F32)
        pend_ref[0] = 0
        pend_ref[1] = 0

    cnt = cnt_ref[s * N_EXPERTS + e]
    off = off_ref[s * N_EXPERTS + e]

    def scatter_pending():
        base = pend_ref[0]
        rem = pend_ref[1]
        for g in range(tmr // SCATTER_BATCH):
            rows, news = [], []
            for mi in range(g * SCATTER_BATCH, (g + 1) * SCATTER_BATCH):
                tok = jnp.where(mi < rem, idx_ref[0, 0, base + mi], tb)
                wgt = wl_ref[0, 0, base + mi]
                r0 = pl.multiple_of(tok * ROW_CHUNKS, ROW_CHUNKS)
                rows.append(r0)
                news.append(out_ref[pl.ds(r0, ROW_CHUNKS), :] + wgt * yt_ref[pl.ds(mi, ROW_CHUNKS, stride=ss), :])
            for r0, new in zip(rows, news):
                out_ref[pl.ds(r0, ROW_CHUNKS), :] = new

    def expert_tile(base, rem, rows):
        scatter_pending()
        for mi in range(rows):
            tok = idx_ref[0, 0, base + mi]
            slab = src_ref[pl.ds(pl.multiple_of(tok * ROW_CHUNKS, ROW_CHUNKS), ROW_CHUNKS), :]
            xt_ref[pl.ds(mi, ROW_CHUNKS, stride=ss), :] = slab
        x = jnp.concatenate([xt_ref[j * ss:j * ss + rows, :] for j in range(ROW_CHUNKS)], axis=1).astype(BF16)
        gu = _dot(x, wgu_ref[0, 0]) + bgu_ref[0, 0]
        a = jnp.minimum(gu[:, :D_FF], SWIGLU_LIMIT)
        lin = jnp.clip(gu[:, D_FF:], -SWIGLU_LIMIT, SWIGLU_LIMIT)
        act = a * (1.0 / (1.0 + jnp.exp(-SWIGLU_ALPHA * a))) * (lin + 1.0)
        y = _dot(act.astype(BF16), wdn_ref[0, 0]) + bdn_ref[0, 0]
        for j in range(ROW_CHUNKS):
            yt_ref[j * ss:j * ss + rows, :] = y[:, j * LANES:(j + 1) * LANES]
        pend_ref[0] = base
        pend_ref[1] = jnp.minimum(rem, rows)

    tail = cnt % tmr
    n_full = cnt // tmr + (tail > MOE_TAIL_TILE).astype(jnp.int32)

    def full_tile(t, carry):
        expert_tile(off + t * tmr, cnt - t * tmr, tmr)
        return carry

    lax.fori_loop(0, n_full, full_tile, 0)

    @pl.when((tail > 0) & (tail <= MOE_TAIL_TILE))
    def _():
        expert_tile(off + n_full * tmr, tail, MOE_TAIL_TILE)

    @pl.when(e == N_EXPERTS - 1)
    def _():
        scatter_pending()
        pend_ref[1] = 0


def _moe_super_block(t):
    for tb in (4096, 2048, 1024, 512, 256):
        if t % tb == 0:
            return tb
    raise ValueError(f"token count {t} is not a multiple of {ROW_TILE}")


def _moe(f_rows, top_i, top_w, wgu, bgu, wdn, bdn, layer):
    t = top_i.shape[0]
    tb = _moe_super_block(t)
    n_sb = t // tb
    n_asg = tb * TOP_K
    list_len = n_asg + MOE_ROW_TILE
    ei = top_i[:, :TOP_K].reshape(n_sb, n_asg)
    wi = top_w[:, :TOP_K].reshape(n_sb, n_asg)
    order = jnp.argsort(ei, axis=1, stable=True)
    tok = (order // TOP_K).astype(jnp.int32)
    wl = jnp.take_along_axis(wi, order, axis=1)
    cnt = jnp.sum((ei[:, :, None] == jnp.arange(N_EXPERTS, dtype=jnp.int32)).astype(jnp.int32), axis=1)
    off = jnp.cumsum(cnt, axis=1) - cnt
    tok = jnp.pad(tok, ((0, 0), (0, list_len - n_asg))).reshape(n_sb, 1, list_len)
    wl = jnp.pad(wl, ((0, 0), (0, list_len - n_asg))).reshape(n_sb, 1, list_len)
    out_rows = (tb + ROW_TILE) * ROW_CHUNKS
    grid_spec = pltpu.PrefetchScalarGridSpec(
        num_scalar_prefetch=2,
        grid=(n_sb, N_EXPERTS),
        in_specs=[
            pl.BlockSpec((1, 1, list_len), lambda s, e, c, o: (s, 0, 0), memory_space=pltpu.SMEM),
            pl.BlockSpec((1, 1, list_len), lambda s, e, c, o: (s, 0, 0), memory_space=pltpu.SMEM),
            pl.BlockSpec((tb * ROW_CHUNKS, LANES), lambda s, e, c, o: (s, 0), pipeline_mode=pl.Buffered(1)),
            pl.BlockSpec((1, 1, D_MODEL, 2 * D_FF), lambda s, e, c, o: (layer, e, 0, 0)),
            pl.BlockSpec((1, 1, 1, 2 * D_FF), lambda s, e, c, o: (layer, e, 0, 0)),
            pl.BlockSpec((1, 1, D_FF, D_MODEL), lambda s, e, c, o: (layer, e, 0, 0)),
            pl.BlockSpec((1, 1, 1, D_MODEL), lambda s, e, c, o: (layer, e, 0, 0)),
        ],
        out_specs=pl.BlockSpec((out_rows, LANES), lambda s, e, c, o: (s, 0), pipeline_mode=pl.Buffered(1)),
        scratch_shapes=[pltpu.VMEM((ROW_CHUNKS * MOE_STAGE_STRIDE, LANES), F32)] * 2 + [pltpu.SMEM((2,), jnp.int32)],
    )
    return pl.pallas_call(
        functools.partial(_moe_kernel, tb=tb),
        grid_spec=grid_spec,
        out_shape=jax.ShapeDtypeStruct((n_sb * out_rows, LANES), F32),
        compiler_params=_vmem_limit(60),
    )(cnt.reshape(-1), off.reshape(-1), tok, wl, f_rows, wgu, bgu, wdn, bdn), tb


def _post_kernel(x_ref, y_ref, mod_ref, g_ref, o_ref):
    tm = x_ref.shape[0]
    mod = mod_ref[0]
    y = jnp.concatenate([y_ref[pl.ds(j, tm, stride=ROW_CHUNKS), :] for j in range(ROW_CHUNKS)], axis=1)
    o_ref[...] = x_ref[...] + mod[:, 5 * D_MODEL:6 * D_MODEL] * (_rms(y) * g_ref[...])


def _post(x, y_rows, tb, mods3, mod_row_fn, g, seq):
    t = x.shape[0]
    tm = ROW_TILE
    tiles_per_seq = seq // tm
    tiles_per_sb = tb // tm
    return pl.pallas_call(
        _post_kernel,
        grid=(t // tm,),
        in_specs=[
            pl.BlockSpec((tm, D_MODEL), lambda i: (i, 0)),
            pl.BlockSpec((tm * ROW_CHUNKS, LANES), lambda i: ((i // tiles_per_sb) * (tiles_per_sb + 1) + i % tiles_per_sb, 0)),
            pl.BlockSpec((1, 1, N_MOD * D_MODEL), lambda i: (mod_row_fn(i // tiles_per_seq), 0, 0)),
            pl.BlockSpec((1, D_MODEL), lambda i: (0, 0)),
        ],
        out_specs=pl.BlockSpec((tm, D_MODEL), lambda i: (i, 0)),
        out_shape=jax.ShapeDtypeStruct((t, D_MODEL), F32),
        compiler_params=_vmem_limit(48),
    )(x, y_rows, mods3, g)


def _rope_tables(seq):
    lane = jnp.arange(LANES)
    d = lane % HEAD_DIM
    axis = d // (2 * ROT_FREQS)
    freq = d % ROT_FREQS
    first_half = (d % (2 * ROT_FREQS)) < ROT_FREQS
    inv_freq = ROPE_THETA ** (-jnp.arange(ROT_FREQS, dtype=F32) / ROT_FREQS)
    tpos = jnp.arange(seq)
    pos = jnp.where(axis[None, :] == 0, (tpos // GRID_W)[:, None], (tpos % GRID_W)[:, None]).astype(F32)
    ang = pos * inv_freq[freq][None, :]
    return jnp.cos(ang), jnp.sin(ang) * jnp.where(first_half, -1.0, 1.0)[None, :].astype(F32)


def _dft_tables(n):
    k = (jnp.arange(n, dtype=jnp.int32)[:, None] * jnp.arange(n, dtype=jnp.int32)[None, :]) % n
    ang = k.astype(F32) * (2.0 * math.pi / n)
    return jnp.cos(ang), jnp.sin(ang)


def kernel(x, c, ctx, c_ctx, w_mod, b_mod, g_pre_mix, g_post_mix, g_pre_ffn, g_post_ffn, w_in, g_q, g_k, conv_w,
           g_branch, w_out, w_router, b_router, w_gate_up, b_gate_up, w_down, b_down):
    batch, seq, _ = x.shape
    ctx_len = ctx.shape[1]
    depth = w_mod.shape[0]
    assert seq % ROW_TILE == 0 and ctx_len % ROW_TILE == 0 and batch + 1 <= MOD_ROWS_PAD

    cc = jnp.zeros((MOD_ROWS_PAD, D_MODEL), F32).at[:batch].set(c).at[batch].set(c_ctx)
    mods = _modulation(cc, w_mod, b_mod)
    mods3 = mods.reshape(depth * MOD_ROWS_PAD, 1, N_MOD * D_MODEL)

    cos_l, sin_l = _rope_tables(seq)
    cos_c, sin_c = jnp.ones((ctx_len, LANES), F32), jnp.zeros((ctx_len, LANES), F32)
    cn_l, sn_l = (a.astype(BF16) for a in _dft_tables(seq))
    cn_c, sn_c = (a.astype(BF16) for a in _dft_tables(ctx_len))
    c64, s64 = _dft_tables(FFT_GROUP_DIM)
    eye = jnp.eye(FFT_GROUPS, dtype=F32)
    bdc = jnp.kron(eye, c64).astype(BF16)
    bds = jnp.kron(eye, s64).astype(BF16)

    wgu = w_gate_up.astype(BF16)
    bgu = b_gate_up.reshape(depth, N_EXPERTS, 1, 2 * D_FF)
    wdn = w_down.astype(BF16)
    bdn = b_down.reshape(depth, N_EXPERTS, 1, D_MODEL)

    x_lat = x.reshape(batch * seq, D_MODEL)
    x_ctx = ctx.reshape(batch * ctx_len, D_MODEL)
    for layer in range(depth):
        last = layer == depth - 1
        lat_row = lambda b, layer=layer: layer * MOD_ROWS_PAD + b
        ctx_row = lambda b, layer=layer: layer * MOD_ROWS_PAD + batch
        w_in_l = w_in[layer].astype(BF16)
        gpre = g_pre_mix[layer].reshape(1, D_MODEL)
        gq2 = jnp.tile(g_q[layer], 2).reshape(1, LANES)
        gk2 = jnp.tile(g_k[layer], 2).reshape(1, LANES)
        gbr = g_branch[layer].reshape(1, MIX_W)
        wout = w_out[layer].astype(BF16)
        gpost = g_post_mix[layer].reshape(1, D_MODEL)
        gffn = g_pre_ffn[layer].reshape(1, D_MODEL)
        gpf = g_post_ffn[layer].reshape(1, D_MODEL)
        wr = jnp.zeros((D_MODEL, LANES), F32).at[:, :N_EXPERTS].set(w_router[layer])
        wrh = wr.astype(BF16)
        wrl = (wr - wrh.astype(F32)).astype(BF16)
        br = jnp.zeros((1, LANES), F32).at[0, :N_EXPERTS].set(b_router[layer])

        q, k_lat, v_lat, gb, u, fcs = _inproj(x_lat, mods3, lat_row, gpre, w_in_l, cos_l, sin_l, gq2, gk2,
                                              bdc, bds, seq, kv_only=False)
        if last:
            k_ctx, v_ctx = _inproj(x_ctx, mods3, ctx_row, gpre, w_in_l[:, K_OFF:CB_OFF], cos_c, sin_c, gq2, gk2,
                                   bdc, bds, ctx_len, kv_only=True)
        else:
            q_c, k_ctx, v_ctx, gb_c, u_c, fcs_c = _inproj(x_ctx, mods3, ctx_row, gpre, w_in_l, cos_c, sin_c,
                                                          gq2, gk2, bdc, bds, ctx_len, kv_only=False)
        attn = _attention(q, k_lat, v_lat, seq, k_ctx, v_ctx, ctx_len)
        four = _position_dft(cn_l, sn_l, fcs, seq)
        x_lat, f_rows, ti, tw = _merge(attn, gb, u, four, x_lat, mods3, lat_row, conv_w[layer], gbr, wout, gpost,
                                       gffn, wrh, wrl, br, seq)
        y_rows, tb = _moe(f_rows, ti, tw, wgu, bgu, wdn, bdn, layer)
        x_lat = _post(x_lat, y_rows, tb, mods3, lat_row, gpf, seq)
        if not last:
            attn_c = _attention(q_c, k_ctx, v_ctx, ctx_len)
            four_c = _position_dft(cn_c, sn_c, fcs_c, ctx_len)
            x_ctx, f_rows_c, ti_c, tw_c = _merge(attn_c, gb_c, u_c, four_c, x_ctx, mods3, ctx_row, conv_w[layer],
                                                 gbr, wout, gpost, gffn, wrh, wrl, br, ctx_len)
            y_rows_c, tb_c = _moe(f_rows_c, ti_c, tw_c, wgu, bgu, wdn, bdn, layer)
            x_ctx = _post(x_ctx, y_rows_c, tb_c, mods3, ctx_row, gpf, ctx_len)
    return x_lat.reshape(batch, seq, D_MODEL)
```

```python
import functools
import math

import jax
import jax.numpy as jnp
from jax import lax
from jax.experimental import pallas as pl
from jax.experimental.pallas import tpu as pltpu

D_MODEL = 1024
GRID_W = 64
HEAD_DIM = 64
N_Q_HEADS = 8
N_KV_HEADS = 2
Q_PER_KV = N_Q_HEADS // N_KV_HEADS
ATTN_W = N_Q_HEADS * HEAD_DIM
KV_W = N_KV_HEADS * HEAD_DIM
ROT_FREQS = HEAD_DIM // 4
ROPE_THETA = 10000.0
CONV_W = D_MODEL // 4
CONV_K = 3
FFT_W = D_MODEL // 4
FFT_GROUPS = 4
FFT_GROUP_DIM = FFT_W // FFT_GROUPS
MIX_W = ATTN_W + CONV_W + FFT_W
Q_OFF = 0
K_OFF = Q_OFF + ATTN_W
V_OFF = K_OFF + KV_W
CB_OFF = V_OFF + KV_W
CC_OFF = CB_OFF + CONV_W
CV_OFF = CC_OFF + CONV_W
F_OFF = CV_OFF + CONV_W
IN_W = F_OFF + FFT_W
N_EXPERTS = 32
TOP_K = 4
D_FF = D_MODEL
SWIGLU_LIMIT = 7.0
SWIGLU_ALPHA = 1.702
N_MOD = 6
EPS = 1e-6

LANES = 128
SUBLANES = 8
ROW_CHUNKS = D_MODEL // LANES
MOD_ROWS_PAD = 24
HALO_ROWS = 16

F32 = jnp.float32
BF16 = jnp.bfloat16
HIGHEST = lax.Precision.HIGHEST

Q_SCALE = (HEAD_DIM ** -0.5) * math.log2(math.e)

ROW_TILE = 256
MERGE_ROW_TILE = 512
MOE_ROW_TILE = 256
MOE_PAD_TOKENS = 512
MOE_TAIL_TILE = 128
MOE_STAGE_STRIDE = MOE_ROW_TILE + SUBLANES
SCATTER_BATCH = 4


def _vmem_limit(mib):
    return pltpu.CompilerParams(vmem_limit_bytes=mib * 1024 * 1024)


def _dot(a, b):
    return jnp.dot(a, b, preferred_element_type=F32)


def _mod_kernel(cc_ref, w_ref, b_ref, o_ref):
    cc = cc_ref[...]
    s = cc * (1.0 / (1.0 + jnp.exp(-cc)))
    o_ref[0] = jnp.dot(s, w_ref[0], preferred_element_type=F32, precision=HIGHEST) + b_ref[0]


def _modulation(cc, w_mod, b_mod):
    depth = w_mod.shape[0]
    tn = 1024
    return pl.pallas_call(
        _mod_kernel,
        grid=(depth, N_MOD * D_MODEL // tn),
        in_specs=[
            pl.BlockSpec((MOD_ROWS_PAD, D_MODEL), lambda l, j: (0, 0)),
            pl.BlockSpec((1, D_MODEL, tn), lambda l, j: (l, 0, j)),
            pl.BlockSpec((1, 1, tn), lambda l, j: (l, 0, j)),
        ],
        out_specs=pl.BlockSpec((1, MOD_ROWS_PAD, tn), lambda l, j: (l, 0, j)),
        out_shape=jax.ShapeDtypeStruct((depth, MOD_ROWS_PAD, N_MOD * D_MODEL), F32),
        compiler_params=_vmem_limit(32),
    )(cc, w_mod, b_mod.reshape(depth, 1, N_MOD * D_MODEL))


def _rms(x):
    return x * lax.rsqrt(jnp.mean(x * x, axis=-1, keepdims=True) + EPS)


def _modulated_norm(x, gain, shift, scale):
    ms = jnp.mean(x * x, axis=-1, keepdims=True)
    return x * lax.rsqrt(ms + EPS) * gain * (1.0 + scale) + shift


def _head_pair_norm_rope(xc, gain, cos, sin):
    lane = lax.broadcasted_iota(jnp.int32, xc.shape, 1)
    lo_head = lane < HEAD_DIM
    x2 = xc * xc
    s_all = jnp.sum(x2, axis=-1, keepdims=True)
    s_lo = jnp.sum(jnp.where(lo_head, x2, 0.0), axis=-1, keepdims=True)
    s_hi = s_all - s_lo
    inv = jnp.where(lo_head, lax.rsqrt(s_lo * (1.0 / HEAD_DIM) + EPS), lax.rsqrt(s_hi * (1.0 / HEAD_DIM) + EPS))
    y = xc * inv * gain
    first_half = (lane % (2 * ROT_FREQS)) < ROT_FREQS
    partner = jnp.where(first_half, pltpu.roll(y, LANES - ROT_FREQS, axis=1), pltpu.roll(y, ROT_FREQS, axis=1))
    return y * cos + partner * sin


def _token_major_rows(y_ref, rows):
    return jnp.concatenate([y_ref[pl.ds(j, rows, stride=ROW_CHUNKS), :] for j in range(ROW_CHUNKS)], axis=1)


def _inproj_kernel(x_ref, mod_ref, gpre_ref, w_ref, cos_ref, sin_ref, gq_ref, gk_ref, bdc_ref, bds_ref,
                   *rest, kv_only, fuse_post):
    x = x_ref[...]
    if fuse_post:
        y_ref, pmod_ref, gpf_ref = rest[:3]
        rest = rest[3:]
        y = _token_major_rows(y_ref, x.shape[0])
        x = x + pmod_ref[0][:, 5 * D_MODEL:6 * D_MODEL] * (_rms(y) * gpf_ref[...])
        if not kv_only:
            rest[0][...] = x
            rest = rest[1:]
    out_refs = rest
    mod = mod_ref[0]
    h = _modulated_norm(x, gpre_ref[...], mod[:, 0:D_MODEL], mod[:, D_MODEL:2 * D_MODEL])
    p = _dot(h.astype(BF16), w_ref[...])
    cos = cos_ref[...]
    sin = sin_ref[...]
    if kv_only:
        k_ref, vt_ref = out_refs
        k_off, v_off = 0, KV_W
    else:
        q_ref, k_ref, vt_ref, gb_ref, u_ref, fcs_ref = out_refs
        k_off, v_off = K_OFF, V_OFF
        for c in range(ATTN_W // LANES):
            qc = _head_pair_norm_rope(p[:, c * LANES:(c + 1) * LANES], gq_ref[...], cos, sin)
            q_ref[:, c * LANES:(c + 1) * LANES] = (qc * Q_SCALE).astype(BF16)
        gb_ref[...] = p[:, CB_OFF:CC_OFF].astype(BF16)
        u_ref[...] = (p[:, CC_OFF:CV_OFF] * p[:, CV_OFF:F_OFF]).astype(BF16)
        f = p[:, F_OFF:IN_W].astype(BF16)
        fcs_ref[:, 0:FFT_W] = _dot(f, bdc_ref[...]).astype(BF16)
        fcs_ref[:, FFT_W:2 * FFT_W] = _dot(f, bds_ref[...]).astype(BF16)
    kk = _head_pair_norm_rope(p[:, k_off:k_off + KV_W], gk_ref[...], cos, sin)
    lane = lax.broadcasted_iota(jnp.int32, kk.shape, 1)
    k0_lo = jnp.where(lane < HEAD_DIM, kk, 0.0)
    k1_hi = jnp.where(lane < HEAD_DIM, 0.0, kk)
    k_ref[:, 0 * LANES:1 * LANES] = k0_lo.astype(BF16)
    k_ref[:, 1 * LANES:2 * LANES] = pltpu.roll(k0_lo, HEAD_DIM, axis=1).astype(BF16)
    k_ref[:, 2 * LANES:3 * LANES] = pltpu.roll(k1_hi, HEAD_DIM, axis=1).astype(BF16)
    k_ref[:, 3 * LANES:4 * LANES] = k1_hi.astype(BF16)
    vt_ref[...] = p[:, v_off:v_off + KV_W].T.astype(BF16)


def _inproj(x, mods3, mod_row_fn, gpre, w, cos, sin, gq2, gk2, bdc, bds, seq, kv_only, post=None):
    t = x.shape[0]
    tm = min(MERGE_ROW_TILE, seq)
    tiles_per_seq = seq // tm
    width = w.shape[1]
    row = lambda i: (i, 0)
    const = lambda i: (0, 0)
    in_specs = [
        pl.BlockSpec((tm, D_MODEL), row),
        pl.BlockSpec((1, 1, N_MOD * D_MODEL), lambda i: (mod_row_fn(i // tiles_per_seq), 0, 0)),
        pl.BlockSpec((1, D_MODEL), const),
        pl.BlockSpec((D_MODEL, width), const),
        pl.BlockSpec((tm, LANES), lambda i: (i % tiles_per_seq, 0)),
        pl.BlockSpec((tm, LANES), lambda i: (i % tiles_per_seq, 0)),
        pl.BlockSpec((1, LANES), const),
        pl.BlockSpec((1, LANES), const),
        pl.BlockSpec((FFT_W, FFT_W), const),
        pl.BlockSpec((FFT_W, FFT_W), const),
    ]
    args = [x, mods3, gpre, w, cos, sin, gq2, gk2, bdc, bds]
    kv_shapes = [jax.ShapeDtypeStruct((t, K_VARIANTS_W), BF16), jax.ShapeDtypeStruct((KV_W, t), BF16)]
    kv_specs = [pl.BlockSpec((tm, K_VARIANTS_W), row), pl.BlockSpec((KV_W, tm), lambda i: (0, i))]
    if kv_only:
        out_shape, out_specs = kv_shapes, kv_specs
    else:
        out_shape = ([jax.ShapeDtypeStruct((t, ATTN_W), BF16)] + kv_shapes
                     + [jax.ShapeDtypeStruct((t, CONV_W), BF16)] * 2 + [jax.ShapeDtypeStruct((t, 2 * FFT_W), BF16)])
        out_specs = ([pl.BlockSpec((tm, ATTN_W), row)] + kv_specs
                     + [pl.BlockSpec((tm, CONV_W), row)] * 2 + [pl.BlockSpec((tm, 2 * FFT_W), row)])
    if post is not None:
        y_rows, tb, prev_row_fn, gpf = post
        in_specs += [
            pl.BlockSpec((tm * ROW_CHUNKS, LANES), _moe_out_index(tb, tm)),
            pl.BlockSpec((1, 1, N_MOD * D_MODEL), lambda i: (prev_row_fn(i // tiles_per_seq), 0, 0)),
            pl.BlockSpec((1, D_MODEL), const),
        ]
        args += [y_rows, mods3, gpf]
        if not kv_only:
            out_shape = [jax.ShapeDtypeStruct((t, D_MODEL), F32)] + out_shape
            out_specs = [pl.BlockSpec((tm, D_MODEL), row)] + out_specs
    return pl.pallas_call(
        functools.partial(_inproj_kernel, kv_only=kv_only, fuse_post=post is not None),
        grid=(t // tm,),
        in_specs=in_specs,
        out_specs=out_specs,
        out_shape=out_shape,
        compiler_params=_vmem_limit(48),
    )(*args)


K_VARIANTS_W = 2 * N_KV_HEADS * LANES
ATTN_Q_TILE = 512
ATTN_KEY_CHUNK = 256


def _attn_kernel(q_ref, k_ref, vt_ref, *rest, has_ctx, n_cast):
    if n_cast:
        cast_in, rest = rest[:n_cast], rest[n_cast:]
    if has_ctx:
        kc_ref, vct_ref, rest = rest[0], rest[1], rest[2:]
    o_ref, rest = rest[0], rest[1:]
    if n_cast:
        cast_out, rest = rest[:n_cast], rest[n_cast:]
        for src, dst in zip(cast_in, cast_out):
            dst[...] = src[...].astype(BF16)
    (st_ref,) = rest
    nt_dims = (((1,), (1,)), ((), ()))
    n_lat = k_ref.shape[0]
    n_keys = st_ref.shape[1]
    tq = st_ref.shape[2]
    ck = ATTN_KEY_CHUNK

    def scores(h):
        var = (h // Q_PER_KV) * 2 + h % 2
        qp = q_ref[:, (h // 2) * LANES:(h // 2 + 1) * LANES]
        n_split = 4 if n_lat % (4 * ATTN_KEY_CHUNK) == 0 else 1
        step = n_lat // n_split
        for r in range(n_split):
            st_ref[h % 2, r * step:(r + 1) * step, :] = lax.dot_general(
                k_ref[r * step:(r + 1) * step, var * LANES:(var + 1) * LANES], qp, nt_dims,
                preferred_element_type=F32)
        if has_ctx:
            st_ref[h % 2, n_lat:n_keys, :] = lax.dot_general(kc_ref[:, var * LANES:(var + 1) * LANES], qp, nt_dims,
                                                             preferred_element_type=F32)

    outs = []
    scores(0)
    for h in range(N_Q_HEADS):
        kv = h // Q_PER_KV
        lo, hi = kv * HEAD_DIM, (kv + 1) * HEAD_DIM
        if h + 1 < N_Q_HEADS:
            scores(h + 1)
        slot = h % 2
        n_chunks = n_keys // ck

        def chunk8(c):
            return st_ref[slot, c * ck:(c + 1) * ck, :].reshape(ck // SUBLANES, SUBLANES, tq)

        m8 = jnp.max(chunk8(0), axis=0)
        for c in range(1, n_chunks):
            m8 = jnp.maximum(m8, jnp.max(chunk8(c), axis=0))
        m = jnp.max(m8, axis=0, keepdims=True)
        l8 = None
        ot = None
        for c in range(n_chunks):
            p = jnp.exp2(st_ref[slot, c * ck:(c + 1) * ck, :] - m)
            lc = jnp.sum(p.reshape(ck // SUBLANES, SUBLANES, tq), axis=0)
            if c * ck < n_lat:
                vt = vt_ref[lo:hi, c * ck:(c + 1) * ck]
            else:
                vt = vct_ref[lo:hi, c * ck - n_lat:(c + 1) * ck - n_lat]
            oc = _dot(vt, p.astype(BF16))
            l8 = lc if l8 is None else l8 + lc
            ot = oc if ot is None else ot + oc
        l = jnp.sum(l8, axis=0, keepdims=True)
        outs.append(ot / l)
    o_ref[...] = jnp.concatenate(outs, axis=0).T.astype(BF16)


def _attention(q, k, vt, seq, kc=None, vct=None, ctx_len=None, cast=()):
    t = q.shape[0]
    tq = min(ATTN_Q_TILE, seq)
    tiles_per_seq = seq // tq
    has_ctx = kc is not None
    steps = (t // seq) * tiles_per_seq
    in_specs = [
        pl.BlockSpec((tq, ATTN_W), lambda b, i: (b * tiles_per_seq + i, 0)),
        pl.BlockSpec((seq, K_VARIANTS_W), lambda b, i: (b, 0)),
        pl.BlockSpec((KV_W, seq), lambda b, i: (0, b)),
    ]
    args = [q, k, vt]
    out_specs = [pl.BlockSpec((tq, ATTN_W), lambda b, i: (b * tiles_per_seq + i, 0))]
    out_shape = [jax.ShapeDtypeStruct((t, ATTN_W), BF16)]
    for arr, layer, n in cast:
        slab = n // steps
        assert slab * steps == n and slab % HALO_ROWS == 0
        in_specs.append(pl.BlockSpec((slab, arr.shape[1]),
                                     lambda b, i, layer=layer: (layer * steps + b * tiles_per_seq + i, 0)))
        args.append(arr)
        out_specs.append(pl.BlockSpec((slab, arr.shape[1]), lambda b, i: (b * tiles_per_seq + i, 0)))
        out_shape.append(jax.ShapeDtypeStruct((n, arr.shape[1]), BF16))
    if has_ctx:
        in_specs += [pl.BlockSpec((ctx_len, K_VARIANTS_W), lambda b, i: (b, 0)),
                     pl.BlockSpec((KV_W, ctx_len), lambda b, i: (0, b))]
        args += [kc, vct]
    outs = pl.pallas_call(
        functools.partial(_attn_kernel, has_ctx=has_ctx, n_cast=len(cast)),
        grid=(t // seq, tiles_per_seq),
        in_specs=in_specs,
        out_specs=out_specs,
        out_shape=out_shape,
        scratch_shapes=[pltpu.VMEM((2, seq + (ctx_len if has_ctx else 0), tq), F32)],
        compiler_params=_vmem_limit(56),
    )(*args)
    return outs[0], outs[1:]


def _dft_kernel(cn_ref, sn_ref, fcs_ref, o_ref):
    y = _dot(cn_ref[...], fcs_ref[:, 0:FFT_W]) - _dot(sn_ref[...], fcs_ref[:, FFT_W:2 * FFT_W])
    o_ref[...] = y.astype(BF16)


def _position_dft(cn, sn, fcs, seq):
    t = fcs.shape[0]
    tn = min(seq, 1024)
    tiles_per_seq = seq // tn
    return pl.pallas_call(
        _dft_kernel,
        grid=(tiles_per_seq, t // seq),
        in_specs=[
            pl.BlockSpec((tn, seq), lambda i, b: (i, 0)),
            pl.BlockSpec((tn, seq), lambda i, b: (i, 0)),
            pl.BlockSpec((seq, 2 * FFT_W), lambda i, b: (b, 0)),
        ],
        out_specs=pl.BlockSpec((tn, FFT_W), lambda i, b: (b * tiles_per_seq + i, 0)),
        out_shape=jax.ShapeDtypeStruct((t, FFT_W), BF16),
        compiler_params=_vmem_limit(48),
    )(cn, sn, fcs)


def _split_bf16(x):
    hi = x.astype(BF16)
    return hi, (x - hi.astype(F32)).astype(BF16)


def _merge_kernel(o_ref, gb_ref, u_ref, up_ref, un_ref, four_ref, x_ref, mod_ref, cw_ref, gbr_ref, wout_ref,
                  gpost_ref, gffn_ref, wrh_ref, wrl_ref, br_ref,
                  xn_ref, f_ref, ti_ref, tw_ref, *, tiles_per_seq):
    i = pl.program_id(0)
    tm = x_ref.shape[0]
    mod = mod_ref[0]
    u = u_ref[...].astype(F32)
    rowi = lax.broadcasted_iota(jnp.int32, u.shape, 0)
    first = (i % tiles_per_seq) == 0
    last = (i % tiles_per_seq) == tiles_per_seq - 1
    prev_row = jnp.where(first, 0.0, up_ref[HALO_ROWS - 1:HALO_ROWS, :].astype(F32))
    next_row = jnp.where(last, 0.0, un_ref[0:1, :].astype(F32))
    u_up = jnp.where(rowi == 0, prev_row, pltpu.roll(u, 1, axis=0))
    u_dn = jnp.where(rowi == tm - 1, next_row, pltpu.roll(u, tm - 1, axis=0))
    conv = gb_ref[...].astype(F32) * (u_up * cw_ref[0:1, :] + u * cw_ref[1:2, :] + u_dn * cw_ref[2:3, :])
    merged = jnp.concatenate(
        [_rms(o_ref[...].astype(F32)), _rms(conv), _rms(four_ref[...].astype(F32))], axis=-1) * gbr_ref[...]
    mix = _dot(merged.astype(BF16), wout_ref[...])
    xn = x_ref[...] + mod[:, 2 * D_MODEL:3 * D_MODEL] * (_rms(mix) * gpost_ref[...])
    xn_ref[...] = xn
    f = _modulated_norm(xn, gffn_ref[...], mod[:, 3 * D_MODEL:4 * D_MODEL], mod[:, 4 * D_MODEL:5 * D_MODEL])
    for j in range(ROW_CHUNKS):
        f_ref[pl.ds(j, tm, stride=ROW_CHUNKS), :] = f[:, j * LANES:(j + 1) * LANES]
    f_hi, f_lo = _split_bf16(f)
    logits = _dot(f_hi, wrh_ref[...]) + _dot(f_lo, wrh_ref[...]) + _dot(f_hi, wrl_ref[...]) + br_ref[...]
    lane = lax.broadcasted_iota(jnp.int32, logits.shape, 1)
    neg = jnp.float32(-jnp.inf)
    work = jnp.where(lane < N_EXPERTS, logits, neg)
    top_v, top_i = [], []
    for _ in range(TOP_K):
        m = jnp.max(work, axis=-1, keepdims=True)
        idx = jnp.min(jnp.where(work == m, lane, LANES), axis=-1, keepdims=True)
        top_v.append(m)
        top_i.append(idx)
        work = jnp.where(lane == idx, neg, work)
    ex = [jnp.exp(v - top_v[0]) for v in top_v]
    den = ex[0] + ex[1] + ex[2] + ex[3]
    ti = jnp.zeros(logits.shape, jnp.int32)
    tw = jnp.zeros(logits.shape, F32)
    for kk in range(TOP_K):
        ti = jnp.where(lane == kk, top_i[kk], ti)
        tw = jnp.where(lane == kk, ex[kk] / den, tw)
    ti_ref[...] = ti
    tw_ref[...] = tw


def _merge(o, gb, u, four, x, mods3, mod_row_fn, conv_w, gbr, wout, gpost, gffn, wrh, wrl, br, seq):
    t = x.shape[0]
    tm = min(MERGE_ROW_TILE, seq)
    tiles_per_seq = seq // tm
    sub_per_tile = tm // HALO_ROWS
    n_sub = t // HALO_ROWS
    row = lambda i: (i, 0)
    const = lambda i: (0, 0)
    in_specs = [
        pl.BlockSpec((tm, ATTN_W), row),
        pl.BlockSpec((tm, CONV_W), row),
        pl.BlockSpec((tm, CONV_W), row),
        pl.BlockSpec((HALO_ROWS, CONV_W), lambda i: (jnp.maximum(i * sub_per_tile - 1, 0), 0)),
        pl.BlockSpec((HALO_ROWS, CONV_W), lambda i: (jnp.minimum((i + 1) * sub_per_tile, n_sub - 1), 0)),
        pl.BlockSpec((tm, FFT_W), row),
        pl.BlockSpec((tm, D_MODEL), row),
        pl.BlockSpec((1, 1, N_MOD * D_MODEL), lambda i: (mod_row_fn(i // tiles_per_seq), 0, 0)),
        pl.BlockSpec((CONV_K, CONV_W), const),
        pl.BlockSpec((1, MIX_W), const),
        pl.BlockSpec((MIX_W, D_MODEL), const),
        pl.BlockSpec((1, D_MODEL), const),
        pl.BlockSpec((1, D_MODEL), const),
        pl.BlockSpec((D_MODEL, LANES), const),
        pl.BlockSpec((D_MODEL, LANES), const),
        pl.BlockSpec((1, LANES), const),
    ]
    out_shape = [
        jax.ShapeDtypeStruct((t, D_MODEL), F32),
        jax.ShapeDtypeStruct((t * ROW_CHUNKS, LANES), F32),
        jax.ShapeDtypeStruct((t, LANES), jnp.int32),
        jax.ShapeDtypeStruct((t, LANES), F32),
    ]
    out_specs = [
        pl.BlockSpec((tm, D_MODEL), row),
        pl.BlockSpec((tm * ROW_CHUNKS, LANES), row),
        pl.BlockSpec((tm, LANES), row),
        pl.BlockSpec((tm, LANES), row),
    ]
    return pl.pallas_call(
        functools.partial(_merge_kernel, tiles_per_seq=tiles_per_seq),
        grid=(t // tm,),
        in_specs=in_specs,
        out_specs=out_specs,
        out_shape=out_shape,
        compiler_params=_vmem_limit(48),
    )(o, gb, u, u, u, four, x, mods3, conv_w, gbr, wout, gpost, gffn, wrh, wrl, br)


def _moe_kernel(cnt_ref, off_ref, idx_ref, wl_ref, src_ref, wgu_ref, bgu_ref, wdn_ref, bdn_ref, out_ref,
                xt_ref, yt_ref, pend_ref, *, tb):
    s = pl.program_id(0)
    e = pl.program_id(1)
    tmr = MOE_ROW_TILE
    ss = MOE_STAGE_STRIDE

    @pl.when(e == 0)
    def _():
        out_ref[...] = jnp.zeros(out_ref.shape, F32)
        yt_ref[...] = jnp.zeros(yt_ref.shape, F32)
        pend_ref[0] = 0
        pend_ref[1] = 0

    cnt = cnt_ref[s * N_EXPERTS + e]
    off = off_ref[s * N_EXPERTS + e]

    def scatter_pending():
        base = pend_ref[0]
        rem = pend_ref[1]
        for g in range(tmr // SCATTER_BATCH):
            rows, news = [], []
            for mi in range(g * SCATTER_BATCH, (g + 1) * SCATTER_BATCH):
                tok = jnp.where(mi < rem, idx_ref[0, 0, base + mi], tb)
                wgt = wl_ref[0, 0, base + mi]
                r0 = pl.multiple_of(tok * ROW_CHUNKS, ROW_CHUNKS)
                rows.append(r0)
                news.append(out_ref[pl.ds(r0, ROW_CHUNKS), :] + wgt * yt_ref[pl.ds(mi, ROW_CHUNKS, stride=ss), :])
            for r0, new in zip(rows, news):
                out_ref[pl.ds(r0, ROW_CHUNKS), :] = new

    def expert_tile(base, rem, rows):
        scatter_pending()
        for mi in range(rows):
            tok = idx_ref[0, 0, base + mi]
            slab = src_ref[pl.ds(pl.multiple_of(tok * ROW_CHUNKS, ROW_CHUNKS), ROW_CHUNKS), :]
            xt_ref[pl.ds(mi, ROW_CHUNKS, stride=ss), :] = slab
        x = jnp.concatenate([xt_ref[j * ss:j * ss + rows, :] for j in range(ROW_CHUNKS)], axis=1).astype(BF16)
        gu = _dot(x, wgu_ref[0, 0]) + bgu_ref[0, 0]
        a = jnp.minimum(gu[:, :D_FF], SWIGLU_LIMIT)
        lin = jnp.clip(gu[:, D_FF:], -SWIGLU_LIMIT, SWIGLU_LIMIT)
        act = a * (1.0 / (1.0 + jnp.exp(-SWIGLU_ALPHA * a))) * (lin + 1.0)
        y = _dot(act.astype(BF16), wdn_ref[0, 0]) + bdn_ref[0, 0]
        for j in range(ROW_CHUNKS):
            yt_ref[j * ss:j * ss + rows, :] = y[:, j * LANES:(j + 1) * LANES]
        pend_ref[0] = base
        pend_ref[1] = jnp.minimum(rem, rows)

    tail = cnt % tmr
    n_full = cnt // tmr + (tail > MOE_TAIL_TILE).astype(jnp.int32)

    def full_tile(t, carry):
        expert_tile(off + t * tmr, cnt - t * tmr, tmr)
        return carry

    lax.fori_loop(0, n_full, full_tile, 0)

    @pl.when((tail > 0) & (tail <= MOE_TAIL_TILE))
    def _():
        expert_tile(off + n_full * tmr, tail, MOE_TAIL_TILE)

    @pl.when(e == N_EXPERTS - 1)
    def _():
        scatter_pending()
        pend_ref[1] = 0


def _moe_out_index(tb, tm):
    tiles_per_sb = tb // tm
    stride = (tb + MOE_PAD_TOKENS) // tm
    return lambda i: ((i // tiles_per_sb) * stride + i % tiles_per_sb, 0)


def _moe_super_block(t):
    for tb in (4096, 2048, 1024, 512):
        if t % tb == 0:
            return tb
    raise ValueError(f"token count {t} is not a multiple of {MOE_PAD_TOKENS}")


def _moe(f_rows, top_i, top_w, wgu, bgu, wdn, bdn, layer):
    t = top_i.shape[0]
    tb = _moe_super_block(t)
    n_sb = t // tb
    n_asg = tb * TOP_K
    list_len = n_asg + MOE_ROW_TILE
    ei = top_i[:, :TOP_K].reshape(n_sb, n_asg)
    wi = top_w[:, :TOP_K].reshape(n_sb, n_asg)
    order = jnp.argsort(ei, axis=1, stable=True)
    tok = (order // TOP_K).astype(jnp.int32)
    wl = jnp.take_along_axis(wi, order, axis=1)
    cnt = jnp.sum((ei[:, :, None] == jnp.arange(N_EXPERTS, dtype=jnp.int32)).astype(jnp.int32), axis=1)
    off = jnp.cumsum(cnt, axis=1) - cnt
    tok = jnp.pad(tok, ((0, 0), (0, list_len - n_asg))).reshape(n_sb, 1, list_len)
    wl = jnp.pad(wl, ((0, 0), (0, list_len - n_asg))).reshape(n_sb, 1, list_len)
    out_rows = (tb + MOE_PAD_TOKENS) * ROW_CHUNKS
    grid_spec = pltpu.PrefetchScalarGridSpec(
        num_scalar_prefetch=2,
        grid=(n_sb, N_EXPERTS),
        in_specs=[
            pl.BlockSpec((1, 1, list_len), lambda s, e, c, o: (s, 0, 0), memory_space=pltpu.SMEM),
            pl.BlockSpec((1, 1, list_len), lambda s, e, c, o: (s, 0, 0), memory_space=pltpu.SMEM),
            pl.BlockSpec((tb * ROW_CHUNKS, LANES), lambda s, e, c, o: (s, 0), pipeline_mode=pl.Buffered(1)),
            pl.BlockSpec((1, 1, D_MODEL, 2 * D_FF), lambda s, e, c, o: (layer, e, 0, 0)),
            pl.BlockSpec((1, 1, 1, 2 * D_FF), lambda s, e, c, o: (layer, e, 0, 0)),
            pl.BlockSpec((1, 1, D_FF, D_MODEL), lambda s, e, c, o: (layer, e, 0, 0)),
            pl.BlockSpec((1, 1, 1, D_MODEL), lambda s, e, c, o: (layer, e, 0, 0)),
        ],
        out_specs=pl.BlockSpec((out_rows, LANES), lambda s, e, c, o: (s, 0), pipeline_mode=pl.Buffered(1)),
        scratch_shapes=[pltpu.VMEM((ROW_CHUNKS * MOE_STAGE_STRIDE, LANES), F32)] * 2 + [pltpu.SMEM((2,), jnp.int32)],
    )
    return pl.pallas_call(
        functools.partial(_moe_kernel, tb=tb),
        grid_spec=grid_spec,
        out_shape=jax.ShapeDtypeStruct((n_sb * out_rows, LANES), F32),
        compiler_params=_vmem_limit(60),
    )(cnt.reshape(-1), off.reshape(-1), tok, wl, f_rows, wgu, bgu, wdn, bdn), tb


def _post_kernel(x_ref, y_ref, mod_ref, g_ref, o_ref):
    tm = x_ref.shape[0]
    mod = mod_ref[0]
    o_ref[...] = x_ref[...] + mod[:, 5 * D_MODEL:6 * D_MODEL] * (_rms(_token_major_rows(y_ref, tm)) * g_ref[...])


def _post(x, y_rows, tb, mods3, mod_row_fn, g, seq):
    t = x.shape[0]
    tm = ROW_TILE
    tiles_per_seq = seq // tm
    return pl.pallas_call(
        _post_kernel,
        grid=(t // tm,),
        in_specs=[
            pl.BlockSpec((tm, D_MODEL), lambda i: (i, 0)),
            pl.BlockSpec((tm * ROW_CHUNKS, LANES), _moe_out_index(tb, tm)),
            pl.BlockSpec((1, 1, N_MOD * D_MODEL), lambda i: (mod_row_fn(i // tiles_per_seq), 0, 0)),
            pl.BlockSpec((1, D_MODEL), lambda i: (0, 0)),
        ],
        out_specs=pl.BlockSpec((tm, D_MODEL), lambda i: (i, 0)),
        out_shape=jax.ShapeDtypeStruct((t, D_MODEL), F32),
        compiler_params=_vmem_limit(48),
    )(x, y_rows, mods3, g)


def _rope_tables(seq):
    lane = jnp.arange(LANES)
    d = lane % HEAD_DIM
    axis = d // (2 * ROT_FREQS)
    freq = d % ROT_FREQS
    first_half = (d % (2 * ROT_FREQS)) < ROT_FREQS
    inv_freq = ROPE_THETA ** (-jnp.arange(ROT_FREQS, dtype=F32) / ROT_FREQS)
    tpos = jnp.arange(seq)
    pos = jnp.where(axis[None, :] == 0, (tpos // GRID_W)[:, None], (tpos % GRID_W)[:, None]).astype(F32)
    ang = pos * inv_freq[freq][None, :]
    return jnp.cos(ang), jnp.sin(ang) * jnp.where(first_half, -1.0, 1.0)[None, :].astype(F32)


def _dft_tables(n):
    k = (jnp.arange(n, dtype=jnp.int32)[:, None] * jnp.arange(n, dtype=jnp.int32)[None, :]) % n
    ang = k.astype(F32) * (2.0 * math.pi / n)
    return jnp.cos(ang), jnp.sin(ang)


def kernel(x, c, ctx, c_ctx, w_mod, b_mod, g_pre_mix, g_post_mix, g_pre_ffn, g_post_ffn, w_in, g_q, g_k, conv_w,
           g_branch, w_out, w_router, b_router, w_gate_up, b_gate_up, w_down, b_down):
    batch, seq, _ = x.shape
    ctx_len = ctx.shape[1]
    depth = w_mod.shape[0]
    assert seq % ROW_TILE == 0 and ctx_len % ROW_TILE == 0 and batch + 1 <= MOD_ROWS_PAD

    cc = jnp.zeros((MOD_ROWS_PAD, D_MODEL), F32).at[:batch].set(c).at[batch].set(c_ctx)
    mods = _modulation(cc, w_mod, b_mod)
    mods3 = mods.reshape(depth * MOD_ROWS_PAD, 1, N_MOD * D_MODEL)

    cos_l, sin_l = _rope_tables(seq)
    cos_c, sin_c = jnp.ones((ctx_len, LANES), F32), jnp.zeros((ctx_len, LANES), F32)
    cn_l, sn_l = (a.astype(BF16) for a in _dft_tables(seq))
    cn_c, sn_c = (a.astype(BF16) for a in _dft_tables(ctx_len))
    c64, s64 = _dft_tables(FFT_GROUP_DIM)
    eye = jnp.eye(FFT_GROUPS, dtype=F32)
    bdc = jnp.kron(eye, c64).astype(BF16)
    bds = jnp.kron(eye, s64).astype(BF16)

    n_exp_rows_gu = N_EXPERTS * D_MODEL
    n_exp_rows_dn = N_EXPERTS * D_FF
    wgu_rows = w_gate_up.reshape(depth * n_exp_rows_gu, 2 * D_FF)
    wdn_rows = w_down.reshape(depth * n_exp_rows_dn, D_MODEL)
    bgu = b_gate_up.reshape(depth, N_EXPERTS, 1, 2 * D_FF)
    bdn = b_down.reshape(depth, N_EXPERTS, 1, D_MODEL)

    x_lat = x.reshape(batch * seq, D_MODEL)
    x_ctx = ctx.reshape(batch * ctx_len, D_MODEL)
    post_lat = post_ctx = None
    for layer in range(depth):
        last = layer == depth - 1
        lat_row = lambda b, layer=layer: layer * MOD_ROWS_PAD + b
        ctx_row = lambda b, layer=layer: layer * MOD_ROWS_PAD + batch
        w_in_l = w_in[layer].astype(BF16)
        gpre = g_pre_mix[layer].reshape(1, D_MODEL)
        gq2 = jnp.tile(g_q[layer], 2).reshape(1, LANES)
        gk2 = jnp.tile(g_k[layer], 2).reshape(1, LANES)
        gbr = g_branch[layer].reshape(1, MIX_W)
        wout = w_out[layer].astype(BF16)
        gpost = g_post_mix[layer].reshape(1, D_MODEL)
        gffn = g_pre_ffn[layer].reshape(1, D_MODEL)
        gpf = g_post_ffn[layer].reshape(1, D_MODEL)
        wr = jnp.zeros((D_MODEL, LANES), F32).at[:, :N_EXPERTS].set(w_router[layer])
        wrh = wr.astype(BF16)
        wrl = (wr - wrh.astype(F32)).astype(BF16)
        br = jnp.zeros((1, LANES), F32).at[0, :N_EXPERTS].set(b_router[layer])

        outs = _inproj(x_lat, mods3, lat_row, gpre, w_in_l, cos_l, sin_l, gq2, gk2, bdc, bds, seq, kv_only=False,
                       post=post_lat)
        if post_lat is not None:
            x_lat, outs = outs[0], outs[1:]
        q, k_lat, vt_lat, gb, u, fcs = outs
        if last:
            k_ctx, vt_ctx = _inproj(x_ctx, mods3, ctx_row, gpre, w_in_l[:, K_OFF:CB_OFF], cos_c, sin_c, gq2, gk2,
                                    bdc, bds, ctx_len, kv_only=True, post=post_ctx)
        else:
            q_c, k_ctx, vt_ctx, gb_c, u_c, fcs_c = _inproj(x_ctx, mods3, ctx_row, gpre, w_in_l, cos_c, sin_c,
                                                           gq2, gk2, bdc, bds, ctx_len, kv_only=False)
        attn, (wgu_l, wdn_l) = _attention(q, k_lat, vt_lat, seq, k_ctx, vt_ctx, ctx_len,
                                          cast=((wgu_rows, layer, n_exp_rows_gu), (wdn_rows, layer, n_exp_rows_dn)))
        wgu_l = wgu_l.reshape(1, N_EXPERTS, D_MODEL, 2 * D_FF)
        wdn_l = wdn_l.reshape(1, N_EXPERTS, D_FF, D_MODEL)
        four = _position_dft(cn_l, sn_l, fcs, seq)
        x_lat, f_rows, ti, tw = _merge(attn, gb, u, four, x_lat, mods3, lat_row, conv_w[layer], gbr, wout, gpost,
                                       gffn, wrh, wrl, br, seq)
        y_rows, tb = _moe(f_rows, ti, tw, wgu_l, bgu[layer:layer + 1], wdn_l, bdn[layer:layer + 1], 0)
        post_lat = (y_rows, tb, lat_row, gpf)
        if not last:
            attn_c, _ = _attention(q_c, k_ctx, vt_ctx, ctx_len)
            four_c = _position_dft(cn_c, sn_c, fcs_c, ctx_len)
            x_ctx, f_rows_c, ti_c, tw_c = _merge(attn_c, gb_c, u_c, four_c, x_ctx, mods3, ctx_row, conv_w[layer],
                                                 gbr, wout, gpost, gffn, wrh, wrl, br, ctx_len)
            y_rows_c, tb_c = _moe(f_rows_c, ti_c, tw_c, wgu_l, bgu[layer:layer + 1], wdn_l, bdn[layer:layer + 1], 0)
            post_ctx = (y_rows_c, tb_c, ctx_row, gpf)
    y_rows, tb, row_fn, gpf = post_lat
    x_lat = _post(x_lat, y_rows, tb, mods3, row_fn, gpf, seq)
    return x_lat.reshape(batch, seq, D_MODEL)
```

```python
import functools
import math

import jax
import jax.numpy as jnp
from jax import lax
from jax.experimental import pallas as pl
from jax.experimental.pallas import tpu as pltpu

D_MODEL = 1024
GRID_W = 64
HEAD_DIM = 64
N_Q_HEADS = 8
N_KV_HEADS = 2
Q_PER_KV = N_Q_HEADS // N_KV_HEADS
ATTN_W = N_Q_HEADS * HEAD_DIM
KV_W = N_KV_HEADS * HEAD_DIM
ROT_FREQS = HEAD_DIM // 4
ROPE_THETA = 10000.0
CONV_W = D_MODEL // 4
CONV_K = 3
FFT_W = D_MODEL // 4
FFT_GROUPS = 4
FFT_GROUP_DIM = FFT_W // FFT_GROUPS
MIX_W = ATTN_W + CONV_W + FFT_W
Q_OFF = 0
K_OFF = Q_OFF + ATTN_W
V_OFF = K_OFF + KV_W
CB_OFF = V_OFF + KV_W
CC_OFF = CB_OFF + CONV_W
CV_OFF = CC_OFF + CONV_W
F_OFF = CV_OFF + CONV_W
IN_W = F_OFF + FFT_W
N_EXPERTS = 32
TOP_K = 4
D_FF = D_MODEL
SWIGLU_LIMIT = 7.0
SWIGLU_ALPHA = 1.702
N_MOD = 6
EPS = 1e-6

LANES = 128
SUBLANES = 8
ROW_CHUNKS = D_MODEL // LANES
MOD_ROWS_PAD = 24
HALO_ROWS = 16

F32 = jnp.float32
BF16 = jnp.bfloat16
HIGHEST = lax.Precision.HIGHEST

Q_SCALE = (HEAD_DIM ** -0.5) * math.log2(math.e)

ROW_TILE = 256
MERGE_ROW_TILE = 512
MOE_ROW_TILE = 256
MOE_PAD_TOKENS = 512
MOE_TAIL_TILE = 128
MOE_STAGE_STRIDE = MOE_ROW_TILE + SUBLANES
SCATTER_BATCH = 4


def _vmem_limit(mib):
    return pltpu.CompilerParams(vmem_limit_bytes=mib * 1024 * 1024)


def _dot(a, b):
    return jnp.dot(a, b, preferred_element_type=F32)


def _mod_kernel(cc_ref, w_ref, b_ref, o_ref):
    cc = cc_ref[...]
    s = cc * (1.0 / (1.0 + jnp.exp(-cc)))
    o_ref[0] = jnp.dot(s, w_ref[0], preferred_element_type=F32, precision=HIGHEST) + b_ref[0]


def _modulation(cc, w_mod, b_mod):
    depth = w_mod.shape[0]
    tn = 1024
    return pl.pallas_call(
        _mod_kernel,
        grid=(depth, N_MOD * D_MODEL // tn),
        in_specs=[
            pl.BlockSpec((MOD_ROWS_PAD, D_MODEL), lambda l, j: (0, 0)),
            pl.BlockSpec((1, D_MODEL, tn), lambda l, j: (l, 0, j)),
            pl.BlockSpec((1, 1, tn), lambda l, j: (l, 0, j)),
        ],
        out_specs=pl.BlockSpec((1, MOD_ROWS_PAD, tn), lambda l, j: (l, 0, j)),
        out_shape=jax.ShapeDtypeStruct((depth, MOD_ROWS_PAD, N_MOD * D_MODEL), F32),
        compiler_params=_vmem_limit(32),
    )(cc, w_mod, b_mod.reshape(depth, 1, N_MOD * D_MODEL))


def _rms(x):
    return x * lax.rsqrt(jnp.mean(x * x, axis=-1, keepdims=True) + EPS)


def _modulated_norm(x, gain, shift, scale):
    ms = jnp.mean(x * x, axis=-1, keepdims=True)
    return x * lax.rsqrt(ms + EPS) * gain * (1.0 + scale) + shift


def _head_pair_norm_rope(xc, gain, cos, sin):
    lane = lax.broadcasted_iota(jnp.int32, xc.shape, 1)
    lo_head = lane < HEAD_DIM
    x2 = xc * xc
    s_all = jnp.sum(x2, axis=-1, keepdims=True)
    s_lo = jnp.sum(jnp.where(lo_head, x2, 0.0), axis=-1, keepdims=True)
    s_hi = s_all - s_lo
    inv = jnp.where(lo_head, lax.rsqrt(s_lo * (1.0 / HEAD_DIM) + EPS), lax.rsqrt(s_hi * (1.0 / HEAD_DIM) + EPS))
    y = xc * inv * gain
    first_half = (lane % (2 * ROT_FREQS)) < ROT_FREQS
    partner = jnp.where(first_half, pltpu.roll(y, LANES - ROT_FREQS, axis=1), pltpu.roll(y, ROT_FREQS, axis=1))
    return y * cos + partner * sin


def _token_major_rows(y_ref, rows):
    return jnp.concatenate([y_ref[pl.ds(j, rows, stride=ROW_CHUNKS), :] for j in range(ROW_CHUNKS)], axis=1)


def _inproj_kernel(x_ref, mod_ref, gpre_ref, w_ref, cos_ref, sin_ref, gq_ref, gk_ref, bdc_ref, bds_ref,
                   *rest, kv_only, fuse_post):
    x = x_ref[...]
    if fuse_post:
        y_ref, pmod_ref, gpf_ref = rest[:3]
        rest = rest[3:]
        y = _token_major_rows(y_ref, x.shape[0])
        x = x + pmod_ref[0][:, 5 * D_MODEL:6 * D_MODEL] * (_rms(y) * gpf_ref[...])
        if not kv_only:
            rest[0][...] = x
            rest = rest[1:]
    out_refs = rest
    mod = mod_ref[0]
    h = _modulated_norm(x, gpre_ref[...], mod[:, 0:D_MODEL], mod[:, D_MODEL:2 * D_MODEL])
    p = _dot(h.astype(BF16), w_ref[...])
    cos = cos_ref[...]
    sin = sin_ref[...]
    if kv_only:
        k_ref, vt_ref = out_refs
        k_off, v_off = 0, KV_W
    else:
        q_ref, k_ref, vt_ref, gb_ref, u_ref, fcs_ref = out_refs
        k_off, v_off = K_OFF, V_OFF
        for c in range(ATTN_W // LANES):
            qc = _head_pair_norm_rope(p[:, c * LANES:(c + 1) * LANES], gq_ref[...], cos, sin)
            q_ref[:, c * LANES:(c + 1) * LANES] = (qc * Q_SCALE).astype(BF16)
        gb_ref[...] = p[:, CB_OFF:CC_OFF].astype(BF16)
        u_ref[...] = (p[:, CC_OFF:CV_OFF] * p[:, CV_OFF:F_OFF]).astype(BF16)
        f = p[:, F_OFF:IN_W].astype(BF16)
        fcs_ref[:, 0:FFT_W] = _dot(f, bdc_ref[...]).astype(BF16)
        fcs_ref[:, FFT_W:2 * FFT_W] = _dot(f, bds_ref[...]).astype(BF16)
    kk = _head_pair_norm_rope(p[:, k_off:k_off + KV_W], gk_ref[...], cos, sin)
    lane = lax.broadcasted_iota(jnp.int32, kk.shape, 1)
    k0_lo = jnp.where(lane < HEAD_DIM, kk, 0.0)
    k1_hi = jnp.where(lane < HEAD_DIM, 0.0, kk)
    k_ref[:, 0 * LANES:1 * LANES] = k0_lo.astype(BF16)
    k_ref[:, 1 * LANES:2 * LANES] = pltpu.roll(k0_lo, HEAD_DIM, axis=1).astype(BF16)
    k_ref[:, 2 * LANES:3 * LANES] = pltpu.roll(k1_hi, HEAD_DIM, axis=1).astype(BF16)
    k_ref[:, 3 * LANES:4 * LANES] = k1_hi.astype(BF16)
    vt_ref[...] = p[:, v_off:v_off + KV_W].T.astype(BF16)


def _inproj(x, mods3, mod_row_fn, gpre, w, cos, sin, gq2, gk2, bdc, bds, seq, kv_only, post=None):
    t = x.shape[0]
    tm = min(MERGE_ROW_TILE, seq)
    tiles_per_seq = seq // tm
    width = w.shape[1]
    row = lambda i: (i, 0)
    const = lambda i: (0, 0)
    in_specs = [
        pl.BlockSpec((tm, D_MODEL), row),
        pl.BlockSpec((1, 1, N_MOD * D_MODEL), lambda i: (mod_row_fn(i // tiles_per_seq), 0, 0)),
        pl.BlockSpec((1, D_MODEL), const),
        pl.BlockSpec((D_MODEL, width), const),
        pl.BlockSpec((tm, LANES), lambda i: (i % tiles_per_seq, 0)),
        pl.BlockSpec((tm, LANES), lambda i: (i % tiles_per_seq, 0)),
        pl.BlockSpec((1, LANES), const),
        pl.BlockSpec((1, LANES), const),
        pl.BlockSpec((FFT_W, FFT_W), const),
        pl.BlockSpec((FFT_W, FFT_W), const),
    ]
    args = [x, mods3, gpre, w, cos, sin, gq2, gk2, bdc, bds]
    kv_shapes = [jax.ShapeDtypeStruct((t, K_VARIANTS_W), BF16), jax.ShapeDtypeStruct((KV_W, t), BF16)]
    kv_specs = [pl.BlockSpec((tm, K_VARIANTS_W), row), pl.BlockSpec((KV_W, tm), lambda i: (0, i))]
    if kv_only:
        out_shape, out_specs = kv_shapes, kv_specs
    else:
        out_shape = ([jax.ShapeDtypeStruct((t, ATTN_W), BF16)] + kv_shapes
                     + [jax.ShapeDtypeStruct((t, CONV_W), BF16)] * 2 + [jax.ShapeDtypeStruct((t, 2 * FFT_W), BF16)])
        out_specs = ([pl.BlockSpec((tm, ATTN_W), row)] + kv_specs
                     + [pl.BlockSpec((tm, CONV_W), row)] * 2 + [pl.BlockSpec((tm, 2 * FFT_W), row)])
    if post is not None:
        y_rows, tb, prev_row_fn, gpf = post
        in_specs += [
            pl.BlockSpec((tm * ROW_CHUNKS, LANES), _moe_out_index(tb, tm)),
            pl.BlockSpec((1, 1, N_MOD * D_MODEL), lambda i: (prev_row_fn(i // tiles_per_seq), 0, 0)),
            pl.BlockSpec((1, D_MODEL), const),
        ]
        args += [y_rows, mods3, gpf]
        if not kv_only:
            out_shape = [jax.ShapeDtypeStruct((t, D_MODEL), F32)] + out_shape
            out_specs = [pl.BlockSpec((tm, D_MODEL), row)] + out_specs
    return pl.pallas_call(
        functools.partial(_inproj_kernel, kv_only=kv_only, fuse_post=post is not None),
        grid=(t // tm,),
        in_specs=in_specs,
        out_specs=out_specs,
        out_shape=out_shape,
        compiler_params=_vmem_limit(48),
    )(*args)


K_VARIANTS_W = 2 * N_KV_HEADS * LANES
ATTN_Q_TILE = 512
ATTN_KEY_CHUNK = 256


def _attn_kernel(q_ref, k_ref, vt_ref, *rest, has_ctx, n_cast):
    if n_cast:
        cast_in, rest = rest[:n_cast], rest[n_cast:]
    if has_ctx:
        kc_ref, vct_ref, rest = rest[0], rest[1], rest[2:]
    o_ref, rest = rest[0], rest[1:]
    if n_cast:
        cast_out, rest = rest[:n_cast], rest[n_cast:]
        for src, dst in zip(cast_in, cast_out):
            dst[...] = src[...].astype(BF16)
    (st_ref,) = rest
    nt_dims = (((1,), (1,)), ((), ()))
    n_lat = k_ref.shape[0]
    n_keys = st_ref.shape[1]
    tq = st_ref.shape[2]
    ck = ATTN_KEY_CHUNK

    def scores(h):
        var = (h // Q_PER_KV) * 2 + h % 2
        qp = q_ref[:, (h // 2) * LANES:(h // 2 + 1) * LANES]
        n_split = 4 if n_lat % (4 * ATTN_KEY_CHUNK) == 0 else 1
        step = n_lat // n_split
        for r in range(n_split):
            st_ref[h % 2, r * step:(r + 1) * step, :] = lax.dot_general(
                k_ref[r * step:(r + 1) * step, var * LANES:(var + 1) * LANES], qp, nt_dims,
                preferred_element_type=F32)
        if has_ctx:
            st_ref[h % 2, n_lat:n_keys, :] = lax.dot_general(kc_ref[:, var * LANES:(var + 1) * LANES], qp, nt_dims,
                                                             preferred_element_type=F32)

    outs = []
    scores(0)
    for h in range(N_Q_HEADS):
        kv = h // Q_PER_KV
        lo, hi = kv * HEAD_DIM, (kv + 1) * HEAD_DIM
        if h + 1 < N_Q_HEADS:
            scores(h + 1)
        slot = h % 2
        n_chunks = n_keys // ck

        def chunk8(c):
            return st_ref[slot, c * ck:(c + 1) * ck, :].reshape(ck // SUBLANES, SUBLANES, tq)

        m8 = jnp.max(chunk8(0), axis=0)
        for c in range(1, n_chunks):
            m8 = jnp.maximum(m8, jnp.max(chunk8(c), axis=0))
        m = jnp.max(m8, axis=0, keepdims=True)
        l8 = None
        ot = None
        for c in range(n_chunks):
            p = jnp.exp2(st_ref[slot, c * ck:(c + 1) * ck, :] - m)
            lc = jnp.sum(p.reshape(ck // SUBLANES, SUBLANES, tq), axis=0)
            if c * ck < n_lat:
                vt = vt_ref[lo:hi, c * ck:(c + 1) * ck]
            else:
                vt = vct_ref[lo:hi, c * ck - n_lat:(c + 1) * ck - n_lat]
            oc = _dot(vt, p.astype(BF16))
            l8 = lc if l8 is None else l8 + lc
            ot = oc if ot is None else ot + oc
        l = jnp.sum(l8, axis=0, keepdims=True)
        outs.append(ot / l)
    o_ref[...] = jnp.concatenate(outs, axis=0).T.astype(BF16)


def _attention(q, k, vt, seq, kc=None, vct=None, ctx_len=None, cast=()):
    t = q.shape[0]
    tq = min(ATTN_Q_TILE, seq)
    tiles_per_seq = seq // tq
    has_ctx = kc is not None
    steps = (t // seq) * tiles_per_seq
    in_specs = [
        pl.BlockSpec((tq, ATTN_W), lambda b, i: (b * tiles_per_seq + i, 0)),
        pl.BlockSpec((seq, K_VARIANTS_W), lambda b, i: (b, 0)),
        pl.BlockSpec((KV_W, seq), lambda b, i: (0, b)),
    ]
    args = [q, k, vt]
    out_specs = [pl.BlockSpec((tq, ATTN_W), lambda b, i: (b * tiles_per_seq + i, 0))]
    out_shape = [jax.ShapeDtypeStruct((t, ATTN_W), BF16)]
    for arr, layer, n in cast:
        slab = n // steps
        assert slab * steps == n and slab % HALO_ROWS == 0
        in_specs.append(pl.BlockSpec((slab, arr.shape[1]),
                                     lambda b, i, layer=layer: (layer * steps + b * tiles_per_seq + i, 0)))
        args.append(arr)
        out_specs.append(pl.BlockSpec((slab, arr.shape[1]), lambda b, i: (b * tiles_per_seq + i, 0)))
        out_shape.append(jax.ShapeDtypeStruct((n, arr.shape[1]), BF16))
    if has_ctx:
        in_specs += [pl.BlockSpec((ctx_len, K_VARIANTS_W), lambda b, i: (b, 0)),
                     pl.BlockSpec((KV_W, ctx_len), lambda b, i: (0, b))]
        args += [kc, vct]
    outs = pl.pallas_call(
        functools.partial(_attn_kernel, has_ctx=has_ctx, n_cast=len(cast)),
        grid=(t // seq, tiles_per_seq),
        in_specs=in_specs,
        out_specs=out_specs,
        out_shape=out_shape,
        scratch_shapes=[pltpu.VMEM((2, seq + (ctx_len if has_ctx else 0), tq), F32)],
        compiler_params=_vmem_limit(56),
    )(*args)
    return outs[0], outs[1:]


def _dft_kernel(cn_ref, sn_ref, fcs_ref, o_ref):
    y = _dot(cn_ref[...], fcs_ref[:, 0:FFT_W]) - _dot(sn_ref[...], fcs_ref[:, FFT_W:2 * FFT_W])
    o_ref[...] = y.astype(BF16)


def _position_dft(cn, sn, fcs, seq):
    t = fcs.shape[0]
    tn = min(seq, 1024)
    tiles_per_seq = seq // tn
    return pl.pallas_call(
        _dft_kernel,
        grid=(tiles_per_seq, t // seq),
        in_specs=[
            pl.BlockSpec((tn, seq), lambda i, b: (i, 0)),
            pl.BlockSpec((tn, seq), lambda i, b: (i, 0)),
            pl.BlockSpec((seq, 2 * FFT_W), lambda i, b: (b, 0)),
        ],
        out_specs=pl.BlockSpec((tn, FFT_W), lambda i, b: (b * tiles_per_seq + i, 0)),
        out_shape=jax.ShapeDtypeStruct((t, FFT_W), BF16),
        compiler_params=_vmem_limit(48),
    )(cn, sn, fcs)


def _split_bf16(x):
    hi = x.astype(BF16)
    return hi, (x - hi.astype(F32)).astype(BF16)


def _merge_kernel(o_ref, gb_ref, u_ref, up_ref, un_ref, four_ref, x_ref, mod_ref, cw_ref, gbr_ref, wout_ref,
                  gpost_ref, gffn_ref, wrh_ref, wrl_ref, br_ref,
                  xn_ref, f_ref, ti_ref, tw_ref, *, tiles_per_seq):
    i = pl.program_id(0)
    tm = x_ref.shape[0]
    mod = mod_ref[0]
    u = u_ref[...].astype(F32)
    rowi = lax.broadcasted_iota(jnp.int32, u.shape, 0)
    first = (i % tiles_per_seq) == 0
    last = (i % tiles_per_seq) == tiles_per_seq - 1
    prev_row = jnp.where(first, 0.0, up_ref[HALO_ROWS - 1:HALO_ROWS, :].astype(F32))
    next_row = jnp.where(last, 0.0, un_ref[0:1, :].astype(F32))
    u_up = jnp.where(rowi == 0, prev_row, pltpu.roll(u, 1, axis=0))
    u_dn = jnp.where(rowi == tm - 1, next_row, pltpu.roll(u, tm - 1, axis=0))
    conv = gb_ref[...].astype(F32) * (u_up * cw_ref[0:1, :] + u * cw_ref[1:2, :] + u_dn * cw_ref[2:3, :])
    merged = jnp.concatenate(
        [_rms(o_ref[...].astype(F32)), _rms(conv), _rms(four_ref[...].astype(F32))], axis=-1) * gbr_ref[...]
    mix = _dot(merged.astype(BF16), wout_ref[...])
    xn = x_ref[...] + mod[:, 2 * D_MODEL:3 * D_MODEL] * (_rms(mix) * gpost_ref[...])
    xn_ref[...] = xn
    f = _modulated_norm(xn, gffn_ref[...], mod[:, 3 * D_MODEL:4 * D_MODEL], mod[:, 4 * D_MODEL:5 * D_MODEL])
    for j in range(ROW_CHUNKS):
        f_ref[pl.ds(j, tm, stride=ROW_CHUNKS), :] = f[:, j * LANES:(j + 1) * LANES]
    f_hi, f_lo = _split_bf16(f)
    logits = _dot(f_hi, wrh_ref[...]) + _dot(f_lo, wrh_ref[...]) + _dot(f_hi, wrl_ref[...]) + br_ref[...]
    lane = lax.broadcasted_iota(jnp.int32, logits.shape, 1)
    neg = jnp.float32(-jnp.inf)
    work = jnp.where(lane < N_EXPERTS, logits, neg)
    top_v, top_i = [], []
    for _ in range(TOP_K):
        m = jnp.max(work, axis=-1, keepdims=True)
        idx = jnp.min(jnp.where(work == m, lane, LANES), axis=-1, keepdims=True)
        top_v.append(m)
        top_i.append(idx)
        work = jnp.where(lane == idx, neg, work)
    ex = [jnp.exp(v - top_v[0]) for v in top_v]
    den = ex[0] + ex[1] + ex[2] + ex[3]
    ti = jnp.zeros(logits.shape, jnp.int32)
    tw = jnp.zeros(logits.shape, F32)
    for kk in range(TOP_K):
        ti = jnp.where(lane == kk, top_i[kk], ti)
        tw = jnp.where(lane == kk, ex[kk] / den, tw)
    ti_ref[...] = ti
    tw_ref[...] = tw


def _merge(o, gb, u, four, x, mods3, mod_row_fn, conv_w, gbr, wout, gpost, gffn, wrh, wrl, br, seq):
    t = x.shape[0]
    tm = min(MERGE_ROW_TILE, seq)
    tiles_per_seq = seq // tm
    sub_per_tile = tm // HALO_ROWS
    n_sub = t // HALO_ROWS
    row = lambda i: (i, 0)
    const = lambda i: (0, 0)
    in_specs = [
        pl.BlockSpec((tm, ATTN_W), row),
        pl.BlockSpec((tm, CONV_W), row),
        pl.BlockSpec((tm, CONV_W), row),
        pl.BlockSpec((HALO_ROWS, CONV_W), lambda i: (jnp.maximum(i * sub_per_tile - 1, 0), 0)),
        pl.BlockSpec((HALO_ROWS, CONV_W), lambda i: (jnp.minimum((i + 1) * sub_per_tile, n_sub - 1), 0)),
        pl.BlockSpec((tm, FFT_W), row),
        pl.BlockSpec((tm, D_MODEL), row),
        pl.BlockSpec((1, 1, N_MOD * D_MODEL), lambda i: (mod_row_fn(i // tiles_per_seq), 0, 0)),
        pl.BlockSpec((CONV_K, CONV_W), const),
        pl.BlockSpec((1, MIX_W), const),
        pl.BlockSpec((MIX_W, D_MODEL), const),
        pl.BlockSpec((1, D_MODEL), const),
        pl.BlockSpec((1, D_MODEL), const),
        pl.BlockSpec((D_MODEL, LANES), const),
        pl.BlockSpec((D_MODEL, LANES), const),
        pl.BlockSpec((1, LANES), const),
    ]
    out_shape = [
        jax.ShapeDtypeStruct((t, D_MODEL), F32),
        jax.ShapeDtypeStruct((t * ROW_CHUNKS, LANES), F32),
        jax.ShapeDtypeStruct((t, LANES), jnp.int32),
        jax.ShapeDtypeStruct((t, LANES), F32),
    ]
    out_specs = [
        pl.BlockSpec((tm, D_MODEL), row),
        pl.BlockSpec((tm * ROW_CHUNKS, LANES), row),
        pl.BlockSpec((tm, LANES), row),
        pl.BlockSpec((tm, LANES), row),
    ]
    return pl.pallas_call(
        functools.partial(_merge_kernel, tiles_per_seq=tiles_per_seq),
        grid=(t // tm,),
        in_specs=in_specs,
        out_specs=out_specs,
        out_shape=out_shape,
        compiler_params=_vmem_limit(48),
    )(o, gb, u, u, u, four, x, mods3, conv_w, gbr, wout, gpost, gffn, wrh, wrl, br)


def _moe_kernel(cnt_ref, off_ref, idx_ref, wl_ref, src_ref, wgu_ref, bgu_ref, wdn_ref, bdn_ref, out_ref,
                xt_ref, yt_ref, pend_ref, *, tb):
    s = pl.program_id(0)
    e = pl.program_id(1)
    tmr = MOE_ROW_TILE
    ss = MOE_STAGE_STRIDE

    @pl.when(e == 0)
    def _():
        out_ref[...] = jnp.zeros(out_ref.shape, F32)
        yt_ref[...] = jnp.zeros(yt_ref.shape, F32)
        pend_ref[0] = 0
        pend_ref[1] = 0

    cnt = cnt_ref[s * N_EXPERTS + e]
    off = off_ref[s * N_EXPERTS + e]

    def scatter_pending():
        base = pend_ref[0]
        rem = pend_ref[1]
        for g in range(tmr // SCATTER_BATCH):
            rows, news = [], []
            for mi in range(g * SCATTER_BATCH, (g + 1) * SCATTER_BATCH):
                tok = jnp.where(mi < rem, idx_ref[0, 0, base + mi], tb)
                wgt = wl_ref[0, 0, base + mi]
                r0 = pl.multiple_of(tok * ROW_CHUNKS, ROW_CHUNKS)
                rows.append(r0)
                news.append(out_ref[pl.ds(r0, ROW_CHUNKS), :] + wgt * yt_ref[pl.ds(mi, ROW_CHUNKS, stride=ss), :])
            for r0, new in zip(rows, news):
                out_ref[pl.ds(r0, ROW_CHUNKS), :] = new

    def expert_tile(base, rem, rows):
        scatter_pending()
        for mi in range(rows):
            tok = idx_ref[0, 0, base + mi]
            slab = src_ref[pl.ds(pl.multiple_of(tok * ROW_CHUNKS, ROW_CHUNKS), ROW_CHUNKS), :]
            xt_ref[pl.ds(mi, ROW_CHUNKS, stride=ss), :] = slab
        x = jnp.concatenate([xt_ref[j * ss:j * ss + rows, :] for j in range(ROW_CHUNKS)], axis=1).astype(BF16)
        gu = _dot(x, wgu_ref[0, 0]) + bgu_ref[0, 0]
        a = jnp.minimum(gu[:, :D_FF], SWIGLU_LIMIT)
        lin = jnp.clip(gu[:, D_FF:], -SWIGLU_LIMIT, SWIGLU_LIMIT)
        act = a * (1.0 / (1.0 + jnp.exp(-SWIGLU_ALPHA * a))) * (lin + 1.0)
        y = _dot(act.astype(BF16), wdn_ref[0, 0]) + bdn_ref[0, 0]
        for j in range(ROW_CHUNKS):
            yt_ref[j * ss:j * ss + rows, :] = y[:, j * LANES:(j + 1) * LANES]
        pend_ref[0] = base
        pend_ref[1] = jnp.minimum(rem, rows)

    tail = cnt % tmr
    n_full = cnt // tmr + (tail > MOE_TAIL_TILE).astype(jnp.int32)

    def full_tile(t, carry):
        expert_tile(off + t * tmr, cnt - t * tmr, tmr)
        return carry

    lax.fori_loop(0, n_full, full_tile, 0)

    @pl.when((tail > 0) & (tail <= MOE_TAIL_TILE))
    def _():
        expert_tile(off + n_full * tmr, tail, MOE_TAIL_TILE)

    @pl.when(e == N_EXPERTS - 1)
    def _():
        scatter_pending()
        pend_ref[1] = 0


def _moe_out_index(tb, tm):
    tiles_per_sb = tb // tm
    stride = (tb + MOE_PAD_TOKENS) // tm
    return lambda i: ((i // tiles_per_sb) * stride + i % tiles_per_sb, 0)


def _moe_super_block(t):
    for tb in (4096, 2048, 1024, 512):
        if t % tb == 0:
            return tb
    raise ValueError(f"token count {t} is not a multiple of {MOE_PAD_TOKENS}")


def _routing_plan(top_i, top_w, tb):
    n_sb = top_i.shape[0] // tb
    n_asg = tb * TOP_K
    list_len = n_asg + MOE_ROW_TILE
    ei = top_i[:, :TOP_K].reshape(n_sb, n_asg)
    wi = top_w[:, :TOP_K].reshape(n_sb, n_asg)
    order = jnp.argsort(ei, axis=1, stable=True)
    tok = (order // TOP_K).astype(jnp.int32)
    wl = jnp.take_along_axis(wi, order, axis=1)
    cnt = jnp.sum((ei[:, :, None] == jnp.arange(N_EXPERTS, dtype=jnp.int32)).astype(jnp.int32), axis=1)
    off = jnp.cumsum(cnt, axis=1) - cnt
    tok = jnp.pad(tok, ((0, 0), (0, list_len - n_asg))).reshape(n_sb, 1, list_len)
    wl = jnp.pad(wl, ((0, 0), (0, list_len - n_asg))).reshape(n_sb, 1, list_len)
    return cnt, off, tok, wl


def _moe(f_rows, plan, tb, wgu, bgu, wdn, bdn, layer):
    cnt, off, tok, wl = plan
    n_sb = cnt.shape[0]
    list_len = tok.shape[2]
    out_rows = (tb + MOE_PAD_TOKENS) * ROW_CHUNKS
    grid_spec = pltpu.PrefetchScalarGridSpec(
        num_scalar_prefetch=2,
        grid=(n_sb, N_EXPERTS),
        in_specs=[
            pl.BlockSpec((1, 1, list_len), lambda s, e, c, o: (s, 0, 0), memory_space=pltpu.SMEM),
            pl.BlockSpec((1, 1, list_len), lambda s, e, c, o: (s, 0, 0), memory_space=pltpu.SMEM),
            pl.BlockSpec((tb * ROW_CHUNKS, LANES), lambda s, e, c, o: (s, 0), pipeline_mode=pl.Buffered(1)),
            pl.BlockSpec((1, 1, D_MODEL, 2 * D_FF), lambda s, e, c, o: (layer, e, 0, 0)),
            pl.BlockSpec((1, 1, 1, 2 * D_FF), lambda s, e, c, o: (layer, e, 0, 0)),
            pl.BlockSpec((1, 1, D_FF, D_MODEL), lambda s, e, c, o: (layer, e, 0, 0)),
            pl.BlockSpec((1, 1, 1, D_MODEL), lambda s, e, c, o: (layer, e, 0, 0)),
        ],
        out_specs=pl.BlockSpec((out_rows, LANES), lambda s, e, c, o: (s, 0), pipeline_mode=pl.Buffered(1)),
        scratch_shapes=[pltpu.VMEM((ROW_CHUNKS * MOE_STAGE_STRIDE, LANES), F32)] * 2 + [pltpu.SMEM((2,), jnp.int32)],
    )
    return pl.pallas_call(
        functools.partial(_moe_kernel, tb=tb),
        grid_spec=grid_spec,
        out_shape=jax.ShapeDtypeStruct((n_sb * out_rows, LANES), F32),
        compiler_params=_vmem_limit(60),
    )(cnt.reshape(-1), off.reshape(-1), tok, wl, f_rows, wgu, bgu, wdn, bdn)


def _post_kernel(x_ref, y_ref, mod_ref, g_ref, o_ref):
    tm = x_ref.shape[0]
    mod = mod_ref[0]
    o_ref[...] = x_ref[...] + mod[:, 5 * D_MODEL:6 * D_MODEL] * (_rms(_token_major_rows(y_ref, tm)) * g_ref[...])


def _post(x, y_rows, tb, mods3, mod_row_fn, g, seq):
    t = x.shape[0]
    tm = ROW_TILE
    tiles_per_seq = seq // tm
    return pl.pallas_call(
        _post_kernel,
        grid=(t // tm,),
        in_specs=[
            pl.BlockSpec((tm, D_MODEL), lambda i: (i, 0)),
            pl.BlockSpec((tm * ROW_CHUNKS, LANES), _moe_out_index(tb, tm)),
            pl.BlockSpec((1, 1, N_MOD * D_MODEL), lambda i: (mod_row_fn(i // tiles_per_seq), 0, 0)),
            pl.BlockSpec((1, D_MODEL), lambda i: (0, 0)),
        ],
        out_specs=pl.BlockSpec((tm, D_MODEL), lambda i: (i, 0)),
        out_shape=jax.ShapeDtypeStruct((t, D_MODEL), F32),
        compiler_params=_vmem_limit(48),
    )(x, y_rows, mods3, g)


def _rope_tables(seq):
    lane = jnp.arange(LANES)
    d = lane % HEAD_DIM
    axis = d // (2 * ROT_FREQS)
    freq = d % ROT_FREQS
    first_half = (d % (2 * ROT_FREQS)) < ROT_FREQS
    inv_freq = ROPE_THETA ** (-jnp.arange(ROT_FREQS, dtype=F32) / ROT_FREQS)
    tpos = jnp.arange(seq)
    pos = jnp.where(axis[None, :] == 0, (tpos // GRID_W)[:, None], (tpos % GRID_W)[:, None]).astype(F32)
    ang = pos * inv_freq[freq][None, :]
    return jnp.cos(ang), jnp.sin(ang) * jnp.where(first_half, -1.0, 1.0)[None, :].astype(F32)


def _dft_tables(n):
    k = (jnp.arange(n, dtype=jnp.int32)[:, None] * jnp.arange(n, dtype=jnp.int32)[None, :]) % n
    ang = k.astype(F32) * (2.0 * math.pi / n)
    return jnp.cos(ang), jnp.sin(ang)


def _dft_tables_split(n):
    fine = FFT_GROUP_DIM
    coarse = n // fine
    rows = jnp.arange(n, dtype=jnp.int32)[:, None]
    ang_a = ((rows * (fine * jnp.arange(coarse, dtype=jnp.int32))[None, :]) % n).astype(F32) * (2.0 * math.pi / n)
    ang_b = ((rows * jnp.arange(fine, dtype=jnp.int32)[None, :]) % n).astype(F32) * (2.0 * math.pi / n)
    ca, sa = jnp.cos(ang_a)[:, :, None], jnp.sin(ang_a)[:, :, None]
    cb, sb = jnp.cos(ang_b)[:, None, :], jnp.sin(ang_b)[:, None, :]
    return (ca * cb - sa * sb).reshape(n, n), (sa * cb + ca * sb).reshape(n, n)


def kernel(x, c, ctx, c_ctx, w_mod, b_mod, g_pre_mix, g_post_mix, g_pre_ffn, g_post_ffn, w_in, g_q, g_k, conv_w,
           g_branch, w_out, w_router, b_router, w_gate_up, b_gate_up, w_down, b_down):
    batch, seq, _ = x.shape
    ctx_len = ctx.shape[1]
    depth = w_mod.shape[0]
    assert seq % ROW_TILE == 0 and ctx_len % ROW_TILE == 0 and batch + 1 <= MOD_ROWS_PAD

    cc = jnp.zeros((MOD_ROWS_PAD, D_MODEL), F32).at[:batch].set(c).at[batch].set(c_ctx)
    mods = _modulation(cc, w_mod, b_mod)
    mods3 = mods.reshape(depth * MOD_ROWS_PAD, 1, N_MOD * D_MODEL)

    cos_l, sin_l = _rope_tables(seq)
    cos_c, sin_c = jnp.ones((ctx_len, LANES), F32), jnp.zeros((ctx_len, LANES), F32)
    cn_l, sn_l = (a.astype(BF16) for a in _dft_tables_split(seq))
    cn_c, sn_c = (a.astype(BF16) for a in _dft_tables_split(ctx_len))
    c64, s64 = _dft_tables(FFT_GROUP_DIM)
    eye = jnp.eye(FFT_GROUPS, dtype=F32)
    bdc = jnp.kron(eye, c64).astype(BF16)
    bds = jnp.kron(eye, s64).astype(BF16)

    n_exp_rows_gu = N_EXPERTS * D_MODEL
    n_exp_rows_dn = N_EXPERTS * D_FF
    wgu_rows = w_gate_up.reshape(depth * n_exp_rows_gu, 2 * D_FF)
    wdn_rows = w_down.reshape(depth * n_exp_rows_dn, D_MODEL)
    bgu = b_gate_up.reshape(depth, N_EXPERTS, 1, 2 * D_FF)
    bdn = b_down.reshape(depth, N_EXPERTS, 1, D_MODEL)

    x_lat = x.reshape(batch * seq, D_MODEL)
    x_ctx = ctx.reshape(batch * ctx_len, D_MODEL)
    post_lat = post_ctx = None
    for layer in range(depth):
        last = layer == depth - 1
        lat_row = lambda b, layer=layer: layer * MOD_ROWS_PAD + b
        ctx_row = lambda b, layer=layer: layer * MOD_ROWS_PAD + batch
        w_in_l = w_in[layer].astype(BF16)
        gpre = g_pre_mix[layer].reshape(1, D_MODEL)
        gq2 = jnp.tile(g_q[layer], 2).reshape(1, LANES)
        gk2 = jnp.tile(g_k[layer], 2).reshape(1, LANES)
        gbr = g_branch[layer].reshape(1, MIX_W)
        wout = w_out[layer].astype(BF16)
        gpost = g_post_mix[layer].reshape(1, D_MODEL)
        gffn = g_pre_ffn[layer].reshape(1, D_MODEL)
        gpf = g_post_ffn[layer].reshape(1, D_MODEL)
        wr = jnp.zeros((D_MODEL, LANES), F32).at[:, :N_EXPERTS].set(w_router[layer])
        wrh = wr.astype(BF16)
        wrl = (wr - wrh.astype(F32)).astype(BF16)
        br = jnp.zeros((1, LANES), F32).at[0, :N_EXPERTS].set(b_router[layer])

        outs = _inproj(x_lat, mods3, lat_row, gpre, w_in_l, cos_l, sin_l, gq2, gk2, bdc, bds, seq, kv_only=False,
                       post=post_lat)
        if post_lat is not None:
            x_lat, outs = outs[0], outs[1:]
        q, k_lat, vt_lat, gb, u, fcs = outs
        if last:
            k_ctx, vt_ctx = _inproj(x_ctx, mods3, ctx_row, gpre, w_in_l[:, K_OFF:CB_OFF], cos_c, sin_c, gq2, gk2,
                                    bdc, bds, ctx_len, kv_only=True, post=post_ctx)
        else:
            q_c, k_ctx, vt_ctx, gb_c, u_c, fcs_c = _inproj(x_ctx, mods3, ctx_row, gpre, w_in_l, cos_c, sin_c,
                                                           gq2, gk2, bdc, bds, ctx_len, kv_only=False)
        attn, (wgu_l, wdn_l) = _attention(q, k_lat, vt_lat, seq, k_ctx, vt_ctx, ctx_len,
                                          cast=((wgu_rows, layer, n_exp_rows_gu), (wdn_rows, layer, n_exp_rows_dn)))
        wgu_l = wgu_l.reshape(1, N_EXPERTS, D_MODEL, 2 * D_FF)
        wdn_l = wdn_l.reshape(1, N_EXPERTS, D_FF, D_MODEL)
        four = _position_dft(cn_l, sn_l, fcs, seq)
        x_lat, f_rows, ti, tw = _merge(attn, gb, u, four, x_lat, mods3, lat_row, conv_w[layer], gbr, wout, gpost,
                                       gffn, wrh, wrl, br, seq)
        expert_args = (wgu_l, bgu[layer:layer + 1], wdn_l, bdn[layer:layer + 1], 0)
        if last:
            tb = _moe_super_block(ti.shape[0])
            y_rows = _moe(f_rows, _routing_plan(ti, tw, tb), tb, *expert_args)
        else:
            attn_c, _ = _attention(q_c, k_ctx, vt_ctx, ctx_len)
            four_c = _position_dft(cn_c, sn_c, fcs_c, ctx_len)
            x_ctx, f_rows_c, ti_c, tw_c = _merge(attn_c, gb_c, u_c, four_c, x_ctx, mods3, ctx_row, conv_w[layer],
                                                 gbr, wout, gpost, gffn, wrh, wrl, br, ctx_len)
            tb = _moe_super_block(math.gcd(ti.shape[0], ti_c.shape[0]))
            plan = _routing_plan(jnp.concatenate([ti[:, :TOP_K], ti_c[:, :TOP_K]]),
                                 jnp.concatenate([tw[:, :TOP_K], tw_c[:, :TOP_K]]), tb)
            n_sb_lat = ti.shape[0] // tb
            y_rows = _moe(f_rows, tuple(a[:n_sb_lat] for a in plan), tb, *expert_args)
            y_rows_c = _moe(f_rows_c, tuple(a[n_sb_lat:] for a in plan), tb, *expert_args)
            post_ctx = (y_rows_c, tb, ctx_row, gpf)
        post_lat = (y_rows, tb, lat_row, gpf)
    y_rows, tb, row_fn, gpf = post_lat
    x_lat = _post(x_lat, y_rows, tb, mods3, row_fn, gpf, seq)
    return x_lat.reshape(batch, seq, D_MODEL)
```

```python
import functools
import math

import jax
import jax.numpy as jnp
from jax import lax
from jax.experimental import pallas as pl
from jax.experimental.pallas import tpu as pltpu

D_MODEL = 1024
GRID_W = 64
HEAD_DIM = 64
N_Q_HEADS = 8
N_KV_HEADS = 2
Q_PER_KV = N_Q_HEADS // N_KV_HEADS
ATTN_W = N_Q_HEADS * HEAD_DIM
KV_W = N_KV_HEADS * HEAD_DIM
ROT_FREQS = HEAD_DIM // 4
ROPE_THETA = 10000.0
CONV_W = D_MODEL // 4
CONV_K = 3
FFT_W = D_MODEL // 4
FFT_GROUPS = 4
FFT_GROUP_DIM = FFT_W // FFT_GROUPS
MIX_W = ATTN_W + CONV_W + FFT_W
Q_OFF = 0
K_OFF = Q_OFF + ATTN_W
V_OFF = K_OFF + KV_W
CB_OFF = V_OFF + KV_W
CC_OFF = CB_OFF + CONV_W
CV_OFF = CC_OFF + CONV_W
F_OFF = CV_OFF + CONV_W
IN_W = F_OFF + FFT_W
N_EXPERTS = 32
TOP_K = 4
D_FF = D_MODEL
SWIGLU_LIMIT = 7.0
SWIGLU_ALPHA = 1.702
N_MOD = 6
EPS = 1e-6

LANES = 128
SUBLANES = 8
ROW_CHUNKS = D_MODEL // LANES
MOD_ROWS_PAD = 24
HALO_ROWS = 16

F32 = jnp.float32
BF16 = jnp.bfloat16
HIGHEST = lax.Precision.HIGHEST

Q_SCALE = (HEAD_DIM ** -0.5) * math.log2(math.e)

ROW_TILE = 256
MERGE_ROW_TILE = 512
MOE_ROW_TILE = 256
MOE_PAD_TOKENS = 512
MOE_TAIL_TILE = 128
MOE_STAGE_STRIDE = MOE_ROW_TILE + SUBLANES
SCATTER_BATCH = 4


def _vmem_limit(mib):
    return pltpu.CompilerParams(vmem_limit_bytes=mib * 1024 * 1024)


def _dot(a, b):
    return jnp.dot(a, b, preferred_element_type=F32)


def _mod_kernel(cc_ref, w_ref, b_ref, o_ref):
    cc = cc_ref[...]
    s = cc * (1.0 / (1.0 + jnp.exp(-cc)))
    o_ref[0] = jnp.dot(s, w_ref[0], preferred_element_type=F32, precision=HIGHEST) + b_ref[0]


def _modulation(cc, w_mod, b_mod):
    depth = w_mod.shape[0]
    tn = 1024
    return pl.pallas_call(
        _mod_kernel,
        grid=(depth, N_MOD * D_MODEL // tn),
        in_specs=[
            pl.BlockSpec((MOD_ROWS_PAD, D_MODEL), lambda l, j: (0, 0)),
            pl.BlockSpec((1, D_MODEL, tn), lambda l, j: (l, 0, j)),
            pl.BlockSpec((1, 1, tn), lambda l, j: (l, 0, j)),
        ],
        out_specs=pl.BlockSpec((1, MOD_ROWS_PAD, tn), lambda l, j: (l, 0, j)),
        out_shape=jax.ShapeDtypeStruct((depth, MOD_ROWS_PAD, N_MOD * D_MODEL), F32),
        compiler_params=_vmem_limit(32),
    )(cc, w_mod, b_mod.reshape(depth, 1, N_MOD * D_MODEL))


def _rms(x):
    return x * lax.rsqrt(jnp.mean(x * x, axis=-1, keepdims=True) + EPS)


def _modulated_norm(x, gain, shift, scale):
    ms = jnp.mean(x * x, axis=-1, keepdims=True)
    return x * lax.rsqrt(ms + EPS) * gain * (1.0 + scale) + shift


def _head_pair_norm_rope(xc, gain, cos, sin):
    lane = lax.broadcasted_iota(jnp.int32, xc.shape, 1)
    lo_head = lane < HEAD_DIM
    x2 = xc * xc
    s_all = jnp.sum(x2, axis=-1, keepdims=True)
    s_lo = jnp.sum(jnp.where(lo_head, x2, 0.0), axis=-1, keepdims=True)
    s_hi = s_all - s_lo
    inv = jnp.where(lo_head, lax.rsqrt(s_lo * (1.0 / HEAD_DIM) + EPS), lax.rsqrt(s_hi * (1.0 / HEAD_DIM) + EPS))
    y = xc * inv * gain
    first_half = (lane % (2 * ROT_FREQS)) < ROT_FREQS
    partner = jnp.where(first_half, pltpu.roll(y, LANES - ROT_FREQS, axis=1), pltpu.roll(y, ROT_FREQS, axis=1))
    return y * cos + partner * sin


def _token_major_rows(y_ref, rows):
    return jnp.concatenate([y_ref[pl.ds(j, rows, stride=ROW_CHUNKS), :] for j in range(ROW_CHUNKS)], axis=1)


def _inproj_kernel(x_ref, mod_ref, gpre_ref, w_ref, cos_ref, sin_ref, gq_ref, gk_ref, bdc_ref, bds_ref,
                   *rest, kv_only, fuse_post):
    x = x_ref[...]
    if fuse_post:
        y_ref, pmod_ref, gpf_ref = rest[:3]
        rest = rest[3:]
        y = _token_major_rows(y_ref, x.shape[0])
        x = x + pmod_ref[0][:, 5 * D_MODEL:6 * D_MODEL] * (_rms(y) * gpf_ref[...])
        if not kv_only:
            rest[0][...] = x
            rest = rest[1:]
    out_refs = rest
    mod = mod_ref[0]
    h = _modulated_norm(x, gpre_ref[...], mod[:, 0:D_MODEL], mod[:, D_MODEL:2 * D_MODEL])
    p = _dot(h.astype(BF16), w_ref[...])
    cos = cos_ref[...]
    sin = sin_ref[...]
    if kv_only:
        k_ref, vt_ref = out_refs
        k_off, v_off = 0, KV_W
    else:
        q_ref, k_ref, vt_ref, gb_ref, u_ref, fcs_ref = out_refs
        k_off, v_off = K_OFF, V_OFF
        for c in range(ATTN_W // LANES):
            qc = _head_pair_norm_rope(p[:, c * LANES:(c + 1) * LANES], gq_ref[...], cos, sin)
            q_ref[:, c * LANES:(c + 1) * LANES] = (qc * Q_SCALE).astype(BF16)
        gb_ref[...] = p[:, CB_OFF:CC_OFF].astype(BF16)
        u_ref[...] = (p[:, CC_OFF:CV_OFF] * p[:, CV_OFF:F_OFF]).astype(BF16)
        f = p[:, F_OFF:IN_W].astype(BF16)
        fcs_ref[:, 0:FFT_W] = _dot(f, bdc_ref[...]).astype(BF16)
        fcs_ref[:, FFT_W:2 * FFT_W] = _dot(f, bds_ref[...]).astype(BF16)
    kk = _head_pair_norm_rope(p[:, k_off:k_off + KV_W], gk_ref[...], cos, sin)
    lane = lax.broadcasted_iota(jnp.int32, kk.shape, 1)
    k0_lo = jnp.where(lane < HEAD_DIM, kk, 0.0)
    k1_hi = jnp.where(lane < HEAD_DIM, 0.0, kk)
    k_ref[:, 0 * LANES:1 * LANES] = k0_lo.astype(BF16)
    k_ref[:, 1 * LANES:2 * LANES] = pltpu.roll(k0_lo, HEAD_DIM, axis=1).astype(BF16)
    k_ref[:, 2 * LANES:3 * LANES] = pltpu.roll(k1_hi, HEAD_DIM, axis=1).astype(BF16)
    k_ref[:, 3 * LANES:4 * LANES] = k1_hi.astype(BF16)
    vt_ref[...] = p[:, v_off:v_off + KV_W].T.astype(BF16)


def _inproj(x, mods3, mod_row_fn, gpre, w, cos, sin, gq2, gk2, bdc, bds, seq, kv_only, post=None):
    t = x.shape[0]
    tm = min(MERGE_ROW_TILE, seq)
    tiles_per_seq = seq // tm
    width = w.shape[1]
    row = lambda i: (i, 0)
    const = lambda i: (0, 0)
    in_specs = [
        pl.BlockSpec((tm, D_MODEL), row),
        pl.BlockSpec((1, 1, N_MOD * D_MODEL), lambda i: (mod_row_fn(i // tiles_per_seq), 0, 0)),
        pl.BlockSpec((1, D_MODEL), const),
        pl.BlockSpec((D_MODEL, width), const),
        pl.BlockSpec((tm, LANES), lambda i: (i % tiles_per_seq, 0)),
        pl.BlockSpec((tm, LANES), lambda i: (i % tiles_per_seq, 0)),
        pl.BlockSpec((1, LANES), const),
        pl.BlockSpec((1, LANES), const),
        pl.BlockSpec((FFT_W, FFT_W), const),
        pl.BlockSpec((FFT_W, FFT_W), const),
    ]
    args = [x, mods3, gpre, w, cos, sin, gq2, gk2, bdc, bds]
    kv_shapes = [jax.ShapeDtypeStruct((t, K_VARIANTS_W), BF16), jax.ShapeDtypeStruct((KV_W, t), BF16)]
    kv_specs = [pl.BlockSpec((tm, K_VARIANTS_W), row), pl.BlockSpec((KV_W, tm), lambda i: (0, i))]
    if kv_only:
        out_shape, out_specs = kv_shapes, kv_specs
    else:
        out_shape = ([jax.ShapeDtypeStruct((t, ATTN_W), BF16)] + kv_shapes
                     + [jax.ShapeDtypeStruct((t, CONV_W), BF16)] * 2 + [jax.ShapeDtypeStruct((t, 2 * FFT_W), BF16)])
        out_specs = ([pl.BlockSpec((tm, ATTN_W), row)] + kv_specs
                     + [pl.BlockSpec((tm, CONV_W), row)] * 2 + [pl.BlockSpec((tm, 2 * FFT_W), row)])
    if post is not None:
        y_rows, tb, prev_row_fn, gpf = post
        in_specs += [
            pl.BlockSpec((tm * ROW_CHUNKS, LANES), _moe_out_index(tb, tm)),
            pl.BlockSpec((1, 1, N_MOD * D_MODEL), lambda i: (prev_row_fn(i // tiles_per_seq), 0, 0)),
            pl.BlockSpec((1, D_MODEL), const),
        ]
        args += [y_rows, mods3, gpf]
        if not kv_only:
            out_shape = [jax.ShapeDtypeStruct((t, D_MODEL), F32)] + out_shape
            out_specs = [pl.BlockSpec((tm, D_MODEL), row)] + out_specs
    return pl.pallas_call(
        functools.partial(_inproj_kernel, kv_only=kv_only, fuse_post=post is not None),
        grid=(t // tm,),
        in_specs=in_specs,
        out_specs=out_specs,
        out_shape=out_shape,
        compiler_params=_vmem_limit(48),
    )(*args)


K_VARIANTS_W = 2 * N_KV_HEADS * LANES
ATTN_Q_TILE = 512
ATTN_KEY_CHUNK = 256


def _attn_kernel(q_ref, k_ref, vt_ref, *rest, has_ctx, n_cast):
    if n_cast:
        cast_in, rest = rest[:n_cast], rest[n_cast:]
    if has_ctx:
        kc_ref, vct_ref, rest = rest[0], rest[1], rest[2:]
    o_ref, rest = rest[0], rest[1:]
    if n_cast:
        cast_out, rest = rest[:n_cast], rest[n_cast:]
        for src, dst in zip(cast_in, cast_out):
            dst[...] = src[...].astype(BF16)
    st_ref, p_ref = rest
    nt_dims = (((1,), (1,)), ((), ()))
    n_lat = k_ref.shape[0]
    n_keys = st_ref.shape[1]
    tq = st_ref.shape[2]
    ck = ATTN_KEY_CHUNK

    def scores(h):
        var = (h // Q_PER_KV) * 2 + h % 2
        qp = q_ref[:, (h // 2) * LANES:(h // 2 + 1) * LANES]
        n_split = 4 if n_lat % (4 * ATTN_KEY_CHUNK) == 0 else 1
        step = n_lat // n_split
        for r in range(n_split):
            st_ref[h % 2, r * step:(r + 1) * step, :] = lax.dot_general(
                k_ref[r * step:(r + 1) * step, var * LANES:(var + 1) * LANES], qp, nt_dims,
                preferred_element_type=F32)
        if has_ctx:
            st_ref[h % 2, n_lat:n_keys, :] = lax.dot_general(kc_ref[:, var * LANES:(var + 1) * LANES], qp, nt_dims,
                                                             preferred_element_type=F32)

    outs = []
    scores(0)
    for h in range(N_Q_HEADS):
        kv = h // Q_PER_KV
        lo, hi = kv * HEAD_DIM, (kv + 1) * HEAD_DIM
        if h + 1 < N_Q_HEADS:
            scores(h + 1)
        slot = h % 2
        n_chunks = n_keys // ck

        def chunk8(c):
            return st_ref[slot, c * ck:(c + 1) * ck, :].reshape(ck // SUBLANES, SUBLANES, tq)

        m8 = jnp.max(chunk8(0), axis=0)
        for c in range(1, n_chunks):
            m8 = jnp.maximum(m8, jnp.max(chunk8(c), axis=0))
        m = jnp.max(m8, axis=0, keepdims=True)
        l8 = None
        for c in range(n_chunks):
            p = jnp.exp2(st_ref[slot, c * ck:(c + 1) * ck, :] - m)
            lc = jnp.sum(p.reshape(ck // SUBLANES, SUBLANES, tq), axis=0)
            p_ref[c * ck:(c + 1) * ck, :] = p.astype(BF16)
            l8 = lc if l8 is None else l8 + lc
        l = jnp.sum(l8, axis=0, keepdims=True)
        ot = _dot(vt_ref[lo:hi, :], p_ref[0:n_lat, :])
        if has_ctx:
            ot = ot + _dot(vct_ref[lo:hi, :], p_ref[n_lat:n_keys, :])
        outs.append(ot / l)
    o_ref[...] = jnp.concatenate(outs, axis=0).T.astype(BF16)


def _attention(q, k, vt, seq, kc=None, vct=None, ctx_len=None, cast=()):
    t = q.shape[0]
    tq = min(ATTN_Q_TILE, seq)
    tiles_per_seq = seq // tq
    has_ctx = kc is not None
    steps = (t // seq) * tiles_per_seq
    in_specs = [
        pl.BlockSpec((tq, ATTN_W), lambda b, i: (b * tiles_per_seq + i, 0)),
        pl.BlockSpec((seq, K_VARIANTS_W), lambda b, i: (b, 0)),
        pl.BlockSpec((KV_W, seq), lambda b, i: (0, b)),
    ]
    args = [q, k, vt]
    out_specs = [pl.BlockSpec((tq, ATTN_W), lambda b, i: (b * tiles_per_seq + i, 0))]
    out_shape = [jax.ShapeDtypeStruct((t, ATTN_W), BF16)]
    for arr, layer, n in cast:
        slab = n // steps
        assert slab * steps == n and slab % HALO_ROWS == 0
        in_specs.append(pl.BlockSpec((slab, arr.shape[1]),
                                     lambda b, i, layer=layer: (layer * steps + b * tiles_per_seq + i, 0)))
        args.append(arr)
        out_specs.append(pl.BlockSpec((slab, arr.shape[1]), lambda b, i: (b * tiles_per_seq + i, 0)))
        out_shape.append(jax.ShapeDtypeStruct((n, arr.shape[1]), BF16))
    if has_ctx:
        in_specs += [pl.BlockSpec((ctx_len, K_VARIANTS_W), lambda b, i: (b, 0)),
                     pl.BlockSpec((KV_W, ctx_len), lambda b, i: (0, b))]
        args += [kc, vct]
    outs = pl.pallas_call(
        functools.partial(_attn_kernel, has_ctx=has_ctx, n_cast=len(cast)),
        grid=(t // seq, tiles_per_seq),
        in_specs=in_specs,
        out_specs=out_specs,
        out_shape=out_shape,
        scratch_shapes=[pltpu.VMEM((2, seq + (ctx_len if has_ctx else 0), tq), F32),
                        pltpu.VMEM((seq + (ctx_len if has_ctx else 0), tq), BF16)],
        compiler_params=_vmem_limit(56),
    )(*args)
    return outs[0], outs[1:]


def _dft_kernel(cn_ref, sn_ref, fcs_ref, o_ref):
    y = _dot(cn_ref[...], fcs_ref[:, 0:FFT_W]) - _dot(sn_ref[...], fcs_ref[:, FFT_W:2 * FFT_W])
    o_ref[...] = y.astype(BF16)


def _position_dft(cn, sn, fcs, seq):
    t = fcs.shape[0]
    tn = min(seq, 1024)
    tiles_per_seq = seq // tn
    return pl.pallas_call(
        _dft_kernel,
        grid=(tiles_per_seq, t // seq),
        in_specs=[
            pl.BlockSpec((tn, seq), lambda i, b: (i, 0)),
            pl.BlockSpec((tn, seq), lambda i, b: (i, 0)),
            pl.BlockSpec((seq, 2 * FFT_W), lambda i, b: (b, 0)),
        ],
        out_specs=pl.BlockSpec((tn, FFT_W), lambda i, b: (b * tiles_per_seq + i, 0)),
        out_shape=jax.ShapeDtypeStruct((t, FFT_W), BF16),
        compiler_params=_vmem_limit(48),
    )(cn, sn, fcs)


def _split_bf16(x):
    hi = x.astype(BF16)
    return hi, (x - hi.astype(F32)).astype(BF16)


def _merge_kernel(o_ref, gb_ref, u_ref, up_ref, un_ref, four_ref, x_ref, mod_ref, cw_ref, gbr_ref, wout_ref,
                  gpost_ref, gffn_ref, wrh_ref, wrl_ref, br_ref,
                  xn_ref, f_ref, ti_ref, tw_ref, hist_ref, *, tiles_per_seq):
    i = pl.program_id(0)
    tm = x_ref.shape[0]
    mod = mod_ref[0]
    u = u_ref[...].astype(F32)
    rowi = lax.broadcasted_iota(jnp.int32, u.shape, 0)
    first = (i % tiles_per_seq) == 0
    last = (i % tiles_per_seq) == tiles_per_seq - 1
    prev_row = jnp.where(first, 0.0, up_ref[HALO_ROWS - 1:HALO_ROWS, :].astype(F32))
    next_row = jnp.where(last, 0.0, un_ref[0:1, :].astype(F32))
    u_up = jnp.where(rowi == 0, prev_row, pltpu.roll(u, 1, axis=0))
    u_dn = jnp.where(rowi == tm - 1, next_row, pltpu.roll(u, tm - 1, axis=0))
    conv = gb_ref[...].astype(F32) * (u_up * cw_ref[0:1, :] + u * cw_ref[1:2, :] + u_dn * cw_ref[2:3, :])
    merged = jnp.concatenate(
        [_rms(o_ref[...].astype(F32)), _rms(conv), _rms(four_ref[...].astype(F32))], axis=-1) * gbr_ref[...]
    mix = _dot(merged.astype(BF16), wout_ref[...])
    xn = x_ref[...] + mod[:, 2 * D_MODEL:3 * D_MODEL] * (_rms(mix) * gpost_ref[...])
    xn_ref[...] = xn
    f = _modulated_norm(xn, gffn_ref[...], mod[:, 3 * D_MODEL:4 * D_MODEL], mod[:, 4 * D_MODEL:5 * D_MODEL])
    for j in range(ROW_CHUNKS):
        f_ref[pl.ds(j, tm, stride=ROW_CHUNKS), :] = f[:, j * LANES:(j + 1) * LANES]
    f_hi, f_lo = _split_bf16(f)
    logits = _dot(f_hi, wrh_ref[...]) + _dot(f_lo, wrh_ref[...]) + _dot(f_hi, wrl_ref[...]) + br_ref[...]
    lane = lax.broadcasted_iota(jnp.int32, logits.shape, 1)
    neg = jnp.float32(-jnp.inf)
    work = jnp.where(lane < N_EXPERTS, logits, neg)
    top_v, top_i = [], []
    for _ in range(TOP_K):
        m = jnp.max(work, axis=-1, keepdims=True)
        idx = jnp.min(jnp.where(work == m, lane, LANES), axis=-1, keepdims=True)
        top_v.append(m)
        top_i.append(idx)
        work = jnp.where(lane == idx, neg, work)
    ex = [jnp.exp(v - top_v[0]) for v in top_v]
    den = ex[0] + ex[1] + ex[2] + ex[3]
    ti = jnp.zeros(logits.shape, jnp.int32)
    tw = jnp.zeros(logits.shape, F32)
    for kk in range(TOP_K):
        ti = jnp.where(lane == kk, top_i[kk], ti)
        tw = jnp.where(lane == kk, ex[kk] / den, tw)
    ti_ref[...] = ti
    tw_ref[...] = tw
    hits = jnp.zeros(logits.shape, jnp.int32)
    for kk in range(TOP_K):
        hits = hits + (lane == top_i[kk]).astype(jnp.int32)
    hist_ref[...] = jnp.broadcast_to(jnp.sum(hits, axis=0, keepdims=True), hist_ref.shape)


def _merge(o, gb, u, four, x, mods3, mod_row_fn, conv_w, gbr, wout, gpost, gffn, wrh, wrl, br, seq):
    t = x.shape[0]
    tm = min(2 * MERGE_ROW_TILE, seq)
    tiles_per_seq = seq // tm
    sub_per_tile = tm // HALO_ROWS
    n_sub = t // HALO_ROWS
    row = lambda i: (i, 0)
    const = lambda i: (0, 0)
    in_specs = [
        pl.BlockSpec((tm, ATTN_W), row),
        pl.BlockSpec((tm, CONV_W), row),
        pl.BlockSpec((tm, CONV_W), row),
        pl.BlockSpec((HALO_ROWS, CONV_W), lambda i: (jnp.maximum(i * sub_per_tile - 1, 0), 0)),
        pl.BlockSpec((HALO_ROWS, CONV_W), lambda i: (jnp.minimum((i + 1) * sub_per_tile, n_sub - 1), 0)),
        pl.BlockSpec((tm, FFT_W), row),
        pl.BlockSpec((tm, D_MODEL), row),
        pl.BlockSpec((1, 1, N_MOD * D_MODEL), lambda i: (mod_row_fn(i // tiles_per_seq), 0, 0)),
        pl.BlockSpec((CONV_K, CONV_W), const),
        pl.BlockSpec((1, MIX_W), const),
        pl.BlockSpec((MIX_W, D_MODEL), const),
        pl.BlockSpec((1, D_MODEL), const),
        pl.BlockSpec((1, D_MODEL), const),
        pl.BlockSpec((D_MODEL, LANES), const),
        pl.BlockSpec((D_MODEL, LANES), const),
        pl.BlockSpec((1, LANES), const),
    ]
    out_shape = [
        jax.ShapeDtypeStruct((t, D_MODEL), F32),
        jax.ShapeDtypeStruct((t * ROW_CHUNKS, LANES), F32),
        jax.ShapeDtypeStruct((t, LANES), jnp.int32),
        jax.ShapeDtypeStruct((t, LANES), F32),
        jax.ShapeDtypeStruct((t // tm * SUBLANES, LANES), jnp.int32),
    ]
    out_specs = [
        pl.BlockSpec((tm, D_MODEL), row),
        pl.BlockSpec((tm * ROW_CHUNKS, LANES), row),
        pl.BlockSpec((tm, LANES), row),
        pl.BlockSpec((tm, LANES), row),
        pl.BlockSpec((SUBLANES, LANES), row),
    ]
    return pl.pallas_call(
        functools.partial(_merge_kernel, tiles_per_seq=tiles_per_seq),
        grid=(t // tm,),
        in_specs=in_specs,
        out_specs=out_specs,
        out_shape=out_shape,
        compiler_params=_vmem_limit(48),
    )(o, gb, u, u, u, four, x, mods3, conv_w, gbr, wout, gpost, gffn, wrh, wrl, br)


def _moe_kernel(cnt_ref, off_ref, idx_ref, wl_ref, src_ref, wgu_ref, bgu_ref, wdn_ref, bdn_ref, out_ref,
                xt_ref, yt_ref, pend_ref, *, tb):
    s = pl.program_id(0)
    e = pl.program_id(1)
    tmr = MOE_ROW_TILE
    ss = MOE_STAGE_STRIDE

    @pl.when(e == 0)
    def _():
        out_ref[...] = jnp.zeros(out_ref.shape, F32)
        yt_ref[...] = jnp.zeros(yt_ref.shape, F32)
        pend_ref[0] = 0
        pend_ref[1] = 0

    cnt = cnt_ref[s * N_EXPERTS + e]
    off = off_ref[s * N_EXPERTS + e]

    def scatter_pending():
        base = pend_ref[0]
        rem = pend_ref[1]
        for g in range(tmr // SCATTER_BATCH):
            rows, news = [], []
            for mi in range(g * SCATTER_BATCH, (g + 1) * SCATTER_BATCH):
                tok = jnp.where(mi < rem, idx_ref[0, 0, base + mi], tb * ROW_CHUNKS)
                wgt = wl_ref[0, 0, base + mi]
                r0 = pl.multiple_of(tok, ROW_CHUNKS)
                rows.append(r0)
                news.append(out_ref[pl.ds(r0, ROW_CHUNKS), :] + wgt * yt_ref[pl.ds(mi, ROW_CHUNKS, stride=ss), :])
            for r0, new in zip(rows, news):
                out_ref[pl.ds(r0, ROW_CHUNKS), :] = new

    def expert_tile(base, rem, rows):
        scatter_pending()
        for mi in range(rows):
            tok = idx_ref[0, 0, base + mi]
            slab = src_ref[pl.ds(pl.multiple_of(tok, ROW_CHUNKS), ROW_CHUNKS), :]
            xt_ref[pl.ds(mi, ROW_CHUNKS, stride=ss), :] = slab
        x = jnp.concatenate([xt_ref[j * ss:j * ss + rows, :] for j in range(ROW_CHUNKS)], axis=1).astype(BF16)
        gu = _dot(x, wgu_ref[0, 0]) + bgu_ref[0, 0]
        a = jnp.minimum(gu[:, :D_FF], SWIGLU_LIMIT)
        lin = jnp.clip(gu[:, D_FF:], -SWIGLU_LIMIT, SWIGLU_LIMIT)
        act = a * (1.0 / (1.0 + jnp.exp(-SWIGLU_ALPHA * a))) * (lin + 1.0)
        y = _dot(act.astype(BF16), wdn_ref[0, 0]) + bdn_ref[0, 0]
        for j in range(ROW_CHUNKS):
            yt_ref[j * ss:j * ss + rows, :] = y[:, j * LANES:(j + 1) * LANES]
        pend_ref[0] = base
        pend_ref[1] = jnp.minimum(rem, rows)

    tail = cnt % tmr
    n_full = cnt // tmr + (tail > MOE_TAIL_TILE).astype(jnp.int32)

    def full_tile(t, carry):
        expert_tile(off + t * tmr, cnt - t * tmr, tmr)
        return carry

    lax.fori_loop(0, n_full, full_tile, 0)

    @pl.when((tail > 0) & (tail <= MOE_TAIL_TILE))
    def _():
        expert_tile(off + n_full * tmr, tail, MOE_TAIL_TILE)

    @pl.when(e == N_EXPERTS - 1)
    def _():
        scatter_pending()
        pend_ref[1] = 0


def _moe_out_index(tb, tm):
    tiles_per_sb = tb // tm
    stride = (tb + MOE_PAD_TOKENS) // tm
    return lambda i: ((i // tiles_per_sb) * stride + i % tiles_per_sb, 0)


def _moe_super_block(t):
    for tb in (4096, 2048, 1024, 512):
        if t % tb == 0:
            return tb
    raise ValueError(f"token count {t} is not a multiple of {MOE_PAD_TOKENS}")


def _routing_plan(top_i, top_w, hist, tb):
    n_sb = top_i.shape[0] // tb
    n_asg = tb * TOP_K
    list_len = n_asg + MOE_ROW_TILE
    ei = top_i[:, :TOP_K].reshape(n_sb, n_asg)
    wi = top_w[:, :TOP_K].reshape(n_sb, n_asg)
    order = jnp.argsort(ei, axis=1, stable=True)
    tok = ((order // TOP_K) * ROW_CHUNKS).astype(jnp.int32)
    wl = jnp.take_along_axis(wi, order, axis=1)
    cnt = jnp.sum(hist.reshape(n_sb, -1, LANES), axis=1)[:, :N_EXPERTS]
    off = jnp.cumsum(cnt, axis=1) - cnt
    tok = jnp.pad(tok, ((0, 0), (0, list_len - n_asg))).reshape(n_sb, 1, list_len)
    wl = jnp.pad(wl, ((0, 0), (0, list_len - n_asg))).reshape(n_sb, 1, list_len)
    return cnt, off, tok, wl


def _moe(f_rows, plan, tb, wgu, bgu, wdn, bdn, layer):
    cnt, off, tok, wl = plan
    n_sb = cnt.shape[0]
    list_len = tok.shape[2]
    out_rows = (tb + MOE_PAD_TOKENS) * ROW_CHUNKS
    grid_spec = pltpu.PrefetchScalarGridSpec(
        num_scalar_prefetch=2,
        grid=(n_sb, N_EXPERTS),
        in_specs=[
            pl.BlockSpec((1, 1, list_len), lambda s, e, c, o: (s, 0, 0), memory_space=pltpu.SMEM),
            pl.BlockSpec((1, 1, list_len), lambda s, e, c, o: (s, 0, 0), memory_space=pltpu.SMEM),
            pl.BlockSpec((tb * ROW_CHUNKS, LANES), lambda s, e, c, o: (s, 0), pipeline_mode=pl.Buffered(1)),
            pl.BlockSpec((1, 1, D_MODEL, 2 * D_FF), lambda s, e, c, o: (layer, e, 0, 0)),
            pl.BlockSpec((1, 1, 1, 2 * D_FF), lambda s, e, c, o: (layer, e, 0, 0)),
            pl.BlockSpec((1, 1, D_FF, D_MODEL), lambda s, e, c, o: (layer, e, 0, 0)),
            pl.BlockSpec((1, 1, 1, D_MODEL), lambda s, e, c, o: (layer, e, 0, 0)),
        ],
        out_specs=pl.BlockSpec((out_rows, LANES), lambda s, e, c, o: (s, 0), pipeline_mode=pl.Buffered(1)),
        scratch_shapes=[pltpu.VMEM((ROW_CHUNKS * MOE_STAGE_STRIDE, LANES), F32)] * 2 + [pltpu.SMEM((2,), jnp.int32)],
    )
    return pl.pallas_call(
        functools.partial(_moe_kernel, tb=tb),
        grid_spec=grid_spec,
        out_shape=jax.ShapeDtypeStruct((n_sb * out_rows, LANES), F32),
        compiler_params=_vmem_limit(60),
    )(cnt.reshape(-1), off.reshape(-1), tok, wl, f_rows, wgu, bgu, wdn, bdn)


def _post_kernel(x_ref, y_ref, mod_ref, g_ref, o_ref):
    tm = x_ref.shape[0]
    mod = mod_ref[0]
    o_ref[...] = x_ref[...] + mod[:, 5 * D_MODEL:6 * D_MODEL] * (_rms(_token_major_rows(y_ref, tm)) * g_ref[...])


def _post(x, y_rows, tb, mods3, mod_row_fn, g, seq):
    t = x.shape[0]
    tm = ROW_TILE
    tiles_per_seq = seq // tm
    return pl.pallas_call(
        _post_kernel,
        grid=(t // tm,),
        in_specs=[
            pl.BlockSpec((tm, D_MODEL), lambda i: (i, 0)),
            pl.BlockSpec((tm * ROW_CHUNKS, LANES), _moe_out_index(tb, tm)),
            pl.BlockSpec((1, 1, N_MOD * D_MODEL), lambda i: (mod_row_fn(i // tiles_per_seq), 0, 0)),
            pl.BlockSpec((1, D_MODEL), lambda i: (0, 0)),
        ],
        out_specs=pl.BlockSpec((tm, D_MODEL), lambda i: (i, 0)),
        out_shape=jax.ShapeDtypeStruct((t, D_MODEL), F32),
        compiler_params=_vmem_limit(48),
    )(x, y_rows, mods3, g)


def _rope_tables(seq):
    lane = jnp.arange(LANES)
    d = lane % HEAD_DIM
    axis = d // (2 * ROT_FREQS)
    freq = d % ROT_FREQS
    first_half = (d % (2 * ROT_FREQS)) < ROT_FREQS
    inv_freq = ROPE_THETA ** (-jnp.arange(ROT_FREQS, dtype=F32) / ROT_FREQS)
    tpos = jnp.arange(seq)
    pos = jnp.where(axis[None, :] == 0, (tpos // GRID_W)[:, None], (tpos % GRID_W)[:, None]).astype(F32)
    ang = pos * inv_freq[freq][None, :]
    return jnp.cos(ang), jnp.sin(ang) * jnp.where(first_half, -1.0, 1.0)[None, :].astype(F32)


def _dft_tables(n):
    k = (jnp.arange(n, dtype=jnp.int32)[:, None] * jnp.arange(n, dtype=jnp.int32)[None, :]) % n
    ang = k.astype(F32) * (2.0 * math.pi / n)
    return jnp.cos(ang), jnp.sin(ang)


def _dft_tables_split(n):
    fine = FFT_GROUP_DIM
    coarse = n // fine
    rows = jnp.arange(n, dtype=jnp.int32)[:, None]
    ang_a = ((rows * (fine * jnp.arange(coarse, dtype=jnp.int32))[None, :]) % n).astype(F32) * (2.0 * math.pi / n)
    ang_b = ((rows * jnp.arange(fine, dtype=jnp.int32)[None, :]) % n).astype(F32) * (2.0 * math.pi / n)
    ca, sa = jnp.cos(ang_a)[:, :, None], jnp.sin(ang_a)[:, :, None]
    cb, sb = jnp.cos(ang_b)[:, None, :], jnp.sin(ang_b)[:, None, :]
    return (ca * cb - sa * sb).reshape(n, n), (sa * cb + ca * sb).reshape(n, n)


def kernel(x, c, ctx, c_ctx, w_mod, b_mod, g_pre_mix, g_post_mix, g_pre_ffn, g_post_ffn, w_in, g_q, g_k, conv_w,
           g_branch, w_out, w_router, b_router, w_gate_up, b_gate_up, w_down, b_down):
    batch, seq, _ = x.shape
    ctx_len = ctx.shape[1]
    depth = w_mod.shape[0]
    assert seq % ROW_TILE == 0 and ctx_len % ROW_TILE == 0 and batch + 1 <= MOD_ROWS_PAD

    cc = jnp.zeros((MOD_ROWS_PAD, D_MODEL), F32).at[:batch].set(c).at[batch].set(c_ctx)
    mods = _modulation(cc, w_mod, b_mod)
    mods3 = mods.reshape(depth * MOD_ROWS_PAD, 1, N_MOD * D_MODEL)

    cos_l, sin_l = _rope_tables(seq)
    cos_c, sin_c = jnp.ones((ctx_len, LANES), F32), jnp.zeros((ctx_len, LANES), F32)
    cn_l, sn_l = (a.astype(BF16) for a in _dft_tables_split(seq))
    cn_c, sn_c = (a.astype(BF16) for a in _dft_tables_split(ctx_len))
    c64, s64 = _dft_tables(FFT_GROUP_DIM)
    eye = jnp.eye(FFT_GROUPS, dtype=F32)
    bdc = jnp.kron(eye, c64).astype(BF16)
    bds = jnp.kron(eye, s64).astype(BF16)

    n_exp_rows_gu = N_EXPERTS * D_MODEL
    n_exp_rows_dn = N_EXPERTS * D_FF
    wgu_rows = w_gate_up.reshape(depth * n_exp_rows_gu, 2 * D_FF)
    wdn_rows = w_down.reshape(depth * n_exp_rows_dn, D_MODEL)
    bgu = b_gate_up.reshape(depth, N_EXPERTS, 1, 2 * D_FF)
    bdn = b_down.reshape(depth, N_EXPERTS, 1, D_MODEL)

    x_lat = x.reshape(batch * seq, D_MODEL)
    x_ctx = ctx.reshape(batch * ctx_len, D_MODEL)
    post_lat = post_ctx = None
    for layer in range(depth):
        last = layer == depth - 1
        lat_row = lambda b, layer=layer: layer * MOD_ROWS_PAD + b
        ctx_row = lambda b, layer=layer: layer * MOD_ROWS_PAD + batch
        w_in_l = w_in[layer].astype(BF16)
        gpre = g_pre_mix[layer].reshape(1, D_MODEL)
        gq2 = jnp.tile(g_q[layer], 2).reshape(1, LANES)
        gk2 = jnp.tile(g_k[layer], 2).reshape(1, LANES)
        gbr = g_branch[layer].reshape(1, MIX_W)
        wout = w_out[layer].astype(BF16)
        gpost = g_post_mix[layer].reshape(1, D_MODEL)
        gffn = g_pre_ffn[layer].reshape(1, D_MODEL)
        gpf = g_post_ffn[layer].reshape(1, D_MODEL)
        wr = jnp.zeros((D_MODEL, LANES), F32).at[:, :N_EXPERTS].set(w_router[layer])
        wrh = wr.astype(BF16)
        wrl = (wr - wrh.astype(F32)).astype(BF16)
        br = jnp.zeros((1, LANES), F32).at[0, :N_EXPERTS].set(b_router[layer])

        outs = _inproj(x_lat, mods3, lat_row, gpre, w_in_l, cos_l, sin_l, gq2, gk2, bdc, bds, seq, kv_only=False,
                       post=post_lat)
        if post_lat is not None:
            x_lat, outs = outs[0], outs[1:]
        q, k_lat, vt_lat, gb, u, fcs = outs
        if last:
            k_ctx, vt_ctx = _inproj(x_ctx, mods3, ctx_row, gpre, w_in_l[:, K_OFF:CB_OFF], cos_c, sin_c, gq2, gk2,
                                    bdc, bds, ctx_len, kv_only=True, post=post_ctx)
        else:
            q_c, k_ctx, vt_ctx, gb_c, u_c, fcs_c = _inproj(x_ctx, mods3, ctx_row, gpre, w_in_l, cos_c, sin_c,
                                                           gq2, gk2, bdc, bds, ctx_len, kv_only=False)
        attn, (wgu_l, wdn_l) = _attention(q, k_lat, vt_lat, seq, k_ctx, vt_ctx, ctx_len,
                                          cast=((wgu_rows, layer, n_exp_rows_gu), (wdn_rows, layer, n_exp_rows_dn)))
        wgu_l = wgu_l.reshape(1, N_EXPERTS, D_MODEL, 2 * D_FF)
        wdn_l = wdn_l.reshape(1, N_EXPERTS, D_FF, D_MODEL)
        four = _position_dft(cn_l, sn_l, fcs, seq)
        x_lat, f_rows, ti, tw, hist = _merge(attn, gb, u, four, x_lat, mods3, lat_row, conv_w[layer], gbr, wout, gpost,
                                             gffn, wrh, wrl, br, seq)
        expert_args = (wgu_l, bgu[layer:layer + 1], wdn_l, bdn[layer:layer + 1], 0)
        tb = _moe_super_block(ti.shape[0])
        y_rows = _moe(f_rows, _routing_plan(ti, tw, hist[::SUBLANES], tb), tb, *expert_args)
        post_lat = (y_rows, tb, lat_row, gpf)
        if not last:
            attn_c, _ = _attention(q_c, k_ctx, vt_ctx, ctx_len)
            four_c = _position_dft(cn_c, sn_c, fcs_c, ctx_len)
            x_ctx, f_rows_c, ti_c, tw_c, hist_c = _merge(attn_c, gb_c, u_c, four_c, x_ctx, mods3, ctx_row,
                                                         conv_w[layer], gbr, wout, gpost, gffn, wrh, wrl, br, ctx_len)
            tb_c = _moe_super_block(ti_c.shape[0])
            y_rows_c = _moe(f_rows_c, _routing_plan(ti_c, tw_c, hist_c[::SUBLANES], tb_c), tb_c, *expert_args)
            post_ctx = (y_rows_c, tb_c, ctx_row, gpf)
    y_rows, tb, row_fn, gpf = post_lat
    x_lat = _post(x_lat, y_rows, tb, mods3, row_fn, gpf, seq)
    return x_lat.reshape(batch, seq, D_MODEL)
```

```python
import functools
import math

import jax
import jax.numpy as jnp
from jax import lax
from jax.experimental import pallas as pl
from jax.experimental.pallas import tpu as pltpu

D_MODEL = 1024
GRID_W = 64
HEAD_DIM = 64
N_Q_HEADS = 8
N_KV_HEADS = 2
Q_PER_KV = N_Q_HEADS // N_KV_HEADS
ATTN_W = N_Q_HEADS * HEAD_DIM
KV_W = N_KV_HEADS * HEAD_DIM
ROT_FREQS = HEAD_DIM // 4
ROPE_THETA = 10000.0
CONV_W = D_MODEL // 4
CONV_K = 3
FFT_W = D_MODEL // 4
FFT_GROUPS = 4
FFT_GROUP_DIM = FFT_W // FFT_GROUPS
MIX_W = ATTN_W + CONV_W + FFT_W
Q_OFF = 0
K_OFF = Q_OFF + ATTN_W
V_OFF = K_OFF + KV_W
CB_OFF = V_OFF + KV_W
CC_OFF = CB_OFF + CONV_W
CV_OFF = CC_OFF + CONV_W
F_OFF = CV_OFF + CONV_W
IN_W = F_OFF + FFT_W
N_EXPERTS = 32
TOP_K = 4
D_FF = D_MODEL
SWIGLU_LIMIT = 7.0
SWIGLU_ALPHA = 1.702
N_MOD = 6
EPS = 1e-6

LANES = 128
SUBLANES = 8
ROW_CHUNKS = D_MODEL // LANES
MOD_ROWS_PAD = 24
HALO_ROWS = 16

F32 = jnp.float32
BF16 = jnp.bfloat16
HIGHEST = lax.Precision.HIGHEST

Q_SCALE = (HEAD_DIM ** -0.5) * math.log2(math.e)

ROW_TILE = 256
MERGE_ROW_TILE = 512
MOE_ROW_TILE = 256
MOE_PAD_TOKENS = 1024
MOE_TAIL_TILE = 128
MOE_STAGE_STRIDE = MOE_ROW_TILE + SUBLANES
SCATTER_BATCH = 2


def _vmem_limit(mib):
    return pltpu.CompilerParams(vmem_limit_bytes=mib * 1024 * 1024)


def _dot(a, b):
    return jnp.dot(a, b, preferred_element_type=F32)


def _mod_kernel(cc_ref, w_ref, b_ref, o_ref):
    cc = cc_ref[...]
    s = cc * (1.0 / (1.0 + jnp.exp(-cc)))
    o_ref[0] = jnp.dot(s, w_ref[0], preferred_element_type=F32, precision=HIGHEST) + b_ref[0]


def _modulation(cc, w_mod, b_mod):
    depth = w_mod.shape[0]
    tn = 1024
    return pl.pallas_call(
        _mod_kernel,
        grid=(depth, N_MOD * D_MODEL // tn),
        in_specs=[
            pl.BlockSpec((MOD_ROWS_PAD, D_MODEL), lambda l, j: (0, 0)),
            pl.BlockSpec((1, D_MODEL, tn), lambda l, j: (l, 0, j)),
            pl.BlockSpec((1, 1, tn), lambda l, j: (l, 0, j)),
        ],
        out_specs=pl.BlockSpec((1, MOD_ROWS_PAD, tn), lambda l, j: (l, 0, j)),
        out_shape=jax.ShapeDtypeStruct((depth, MOD_ROWS_PAD, N_MOD * D_MODEL), F32),
        compiler_params=_vmem_limit(32),
    )(cc, w_mod, b_mod.reshape(depth, 1, N_MOD * D_MODEL))


def _rms(x):
    return x * lax.rsqrt(jnp.mean(x * x, axis=-1, keepdims=True) + EPS)


def _modulated_norm(x, gain, shift, scale):
    ms = jnp.mean(x * x, axis=-1, keepdims=True)
    return x * lax.rsqrt(ms + EPS) * gain * (1.0 + scale) + shift


def _head_pair_norm_rope(xc, gain, cos, sin):
    lane = lax.broadcasted_iota(jnp.int32, xc.shape, 1)
    lo_head = lane < HEAD_DIM
    x2 = xc * xc
    s_all = jnp.sum(x2, axis=-1, keepdims=True)
    s_lo = jnp.sum(jnp.where(lo_head, x2, 0.0), axis=-1, keepdims=True)
    s_hi = s_all - s_lo
    inv = jnp.where(lo_head, lax.rsqrt(s_lo * (1.0 / HEAD_DIM) + EPS), lax.rsqrt(s_hi * (1.0 / HEAD_DIM) + EPS))
    y = xc * inv * gain
    first_half = (lane % (2 * ROT_FREQS)) < ROT_FREQS
    partner = jnp.where(first_half, pltpu.roll(y, LANES - ROT_FREQS, axis=1), pltpu.roll(y, ROT_FREQS, axis=1))
    return y * cos + partner * sin


def _token_major_rows(y_ref, rows):
    return jnp.concatenate([y_ref[pl.ds(j, rows, stride=ROW_CHUNKS), :] for j in range(ROW_CHUNKS)], axis=1)


def _inproj_kernel(x_ref, mod_ref, gpre_ref, w_ref, cos_ref, sin_ref, gq_ref, gk_ref, bdc_ref, bds_ref,
                   *rest, kv_only, fuse_post):
    x = x_ref[...]
    if fuse_post:
        y_ref, pmod_ref, gpf_ref = rest[:3]
        rest = rest[3:]
        y = _token_major_rows(y_ref, x.shape[0])
        x = x + pmod_ref[0][:, 5 * D_MODEL:6 * D_MODEL] * (_rms(y) * gpf_ref[...])
        if not kv_only:
            rest[0][...] = x
            rest = rest[1:]
    out_refs = rest
    mod = mod_ref[0]
    h = _modulated_norm(x, gpre_ref[...], mod[:, 0:D_MODEL], mod[:, D_MODEL:2 * D_MODEL])
    p = _dot(h.astype(BF16), w_ref[...])
    cos = cos_ref[...]
    sin = sin_ref[...]
    if kv_only:
        k_ref, vt_ref = out_refs
        k_off, v_off = 0, KV_W
    else:
        q_ref, k_ref, vt_ref, gb_ref, u_ref, fcs_ref = out_refs
        k_off, v_off = K_OFF, V_OFF
        for c in range(ATTN_W // LANES):
            qc = _head_pair_norm_rope(p[:, c * LANES:(c + 1) * LANES], gq_ref[...], cos, sin)
            q_ref[:, c * LANES:(c + 1) * LANES] = (qc * Q_SCALE).astype(BF16)
        gb_ref[...] = p[:, CB_OFF:CC_OFF].astype(BF16)
        u_ref[...] = (p[:, CC_OFF:CV_OFF] * p[:, CV_OFF:F_OFF]).astype(BF16)
        f = p[:, F_OFF:IN_W].astype(BF16)
        fcs_ref[:, 0:FFT_W] = _dot(f, bdc_ref[...]).astype(BF16)
        fcs_ref[:, FFT_W:2 * FFT_W] = _dot(f, bds_ref[...]).astype(BF16)
    kk = _head_pair_norm_rope(p[:, k_off:k_off + KV_W], gk_ref[...], cos, sin)
    lane = lax.broadcasted_iota(jnp.int32, kk.shape, 1)
    k0_lo = jnp.where(lane < HEAD_DIM, kk, 0.0)
    k1_hi = jnp.where(lane < HEAD_DIM, 0.0, kk)
    k_ref[:, 0 * LANES:1 * LANES] = k0_lo.astype(BF16)
    k_ref[:, 1 * LANES:2 * LANES] = pltpu.roll(k0_lo, HEAD_DIM, axis=1).astype(BF16)
    k_ref[:, 2 * LANES:3 * LANES] = pltpu.roll(k1_hi, HEAD_DIM, axis=1).astype(BF16)
    k_ref[:, 3 * LANES:4 * LANES] = k1_hi.astype(BF16)
    vt_ref[...] = p[:, v_off:v_off + KV_W].T.astype(BF16)


def _inproj(x, mods3, mod_row_fn, gpre, w, cos, sin, gq2, gk2, bdc, bds, seq, kv_only, post=None):
    t = x.shape[0]
    tm = min(2 * MERGE_ROW_TILE, seq)
    tiles_per_seq = seq // tm
    width = w.shape[1]
    row = lambda i: (i, 0)
    const = lambda i: (0, 0)
    in_specs = [
        pl.BlockSpec((tm, D_MODEL), row),
        pl.BlockSpec((1, 1, N_MOD * D_MODEL), lambda i: (mod_row_fn(i // tiles_per_seq), 0, 0)),
        pl.BlockSpec((1, D_MODEL), const),
        pl.BlockSpec((D_MODEL, width), const),
        pl.BlockSpec((tm, LANES), lambda i: (i % tiles_per_seq, 0)),
        pl.BlockSpec((tm, LANES), lambda i: (i % tiles_per_seq, 0)),
        pl.BlockSpec((1, LANES), const),
        pl.BlockSpec((1, LANES), const),
        pl.BlockSpec((FFT_W, FFT_W), const),
        pl.BlockSpec((FFT_W, FFT_W), const),
    ]
    args = [x, mods3, gpre, w, cos, sin, gq2, gk2, bdc, bds]
    kv_shapes = [jax.ShapeDtypeStruct((t, K_VARIANTS_W), BF16), jax.ShapeDtypeStruct((KV_W, t), BF16)]
    kv_specs = [pl.BlockSpec((tm, K_VARIANTS_W), row), pl.BlockSpec((KV_W, tm), lambda i: (0, i))]
    if kv_only:
        out_shape, out_specs = kv_shapes, kv_specs
    else:
        out_shape = ([jax.ShapeDtypeStruct((t, ATTN_W), BF16)] + kv_shapes
                     + [jax.ShapeDtypeStruct((t, CONV_W), BF16)] * 2 + [jax.ShapeDtypeStruct((t, 2 * FFT_W), BF16)])
        out_specs = ([pl.BlockSpec((tm, ATTN_W), row)] + kv_specs
                     + [pl.BlockSpec((tm, CONV_W), row)] * 2 + [pl.BlockSpec((tm, 2 * FFT_W), row)])
    if post is not None:
        y_rows, tb, prev_row_fn, gpf = post
        in_specs += [
            pl.BlockSpec((tm * ROW_CHUNKS, LANES), _moe_out_index(tb, tm)),
            pl.BlockSpec((1, 1, N_MOD * D_MODEL), lambda i: (prev_row_fn(i // tiles_per_seq), 0, 0)),
            pl.BlockSpec((1, D_MODEL), const),
        ]
        args += [y_rows, mods3, gpf]
        if not kv_only:
            out_shape = [jax.ShapeDtypeStruct((t, D_MODEL), F32)] + out_shape
            out_specs = [pl.BlockSpec((tm, D_MODEL), row)] + out_specs
    return pl.pallas_call(
        functools.partial(_inproj_kernel, kv_only=kv_only, fuse_post=post is not None),
        grid=(t // tm,),
        in_specs=in_specs,
        out_specs=out_specs,
        out_shape=out_shape,
        compiler_params=_vmem_limit(48),
    )(*args)


K_VARIANTS_W = 2 * N_KV_HEADS * LANES
ATTN_Q_TILE = 512
ATTN_KEY_CHUNK = 256


def _attn_kernel(q_ref, k_ref, vt_ref, *rest, has_ctx, n_cast):
    if n_cast:
        cast_in, rest = rest[:n_cast], rest[n_cast:]
    if has_ctx:
        kc_ref, vct_ref, rest = rest[0], rest[1], rest[2:]
    o_ref, rest = rest[0], rest[1:]
    if n_cast:
        cast_out, rest = rest[:n_cast], rest[n_cast:]
        for src, dst in zip(cast_in, cast_out):
            dst[...] = src[...].astype(BF16)
    st_ref, p_ref = rest
    nt_dims = (((1,), (1,)), ((), ()))
    n_lat = k_ref.shape[0]
    n_keys = st_ref.shape[1]
    tq = st_ref.shape[2]
    ck = ATTN_KEY_CHUNK

    def scores(h):
        var = (h // Q_PER_KV) * 2 + h % 2
        qp = q_ref[:, (h // 2) * LANES:(h // 2 + 1) * LANES]
        n_split = 4 if n_lat % (4 * ATTN_KEY_CHUNK) == 0 else 1
        step = n_lat // n_split
        for r in range(n_split):
            st_ref[h % 2, r * step:(r + 1) * step, :] = lax.dot_general(
                k_ref[r * step:(r + 1) * step, var * LANES:(var + 1) * LANES], qp, nt_dims,
                preferred_element_type=F32)
        if has_ctx:
            st_ref[h % 2, n_lat:n_keys, :] = lax.dot_general(kc_ref[:, var * LANES:(var + 1) * LANES], qp, nt_dims,
                                                             preferred_element_type=F32)

    outs = []
    scores(0)
    for h in range(N_Q_HEADS):
        kv = h // Q_PER_KV
        lo, hi = kv * HEAD_DIM, (kv + 1) * HEAD_DIM
        if h + 1 < N_Q_HEADS:
            scores(h + 1)
        slot = h % 2
        n_chunks = n_keys // ck

        def chunk8(c):
            return st_ref[slot, c * ck:(c + 1) * ck, :].reshape(ck // SUBLANES, SUBLANES, tq)

        m8 = jnp.max(chunk8(0), axis=0)
        for c in range(1, n_chunks):
            m8 = jnp.maximum(m8, jnp.max(chunk8(c), axis=0))
        m = jnp.max(m8, axis=0, keepdims=True)
        l8 = None
        for c in range(n_chunks):
            p = jnp.exp2(st_ref[slot, c * ck:(c + 1) * ck, :] - m)
            lc = jnp.sum(p.reshape(ck // SUBLANES, SUBLANES, tq), axis=0)
            p_ref[c * ck:(c + 1) * ck, :] = p.astype(BF16)
            l8 = lc if l8 is None else l8 + lc
        l = jnp.sum(l8, axis=0, keepdims=True)
        ot = _dot(vt_ref[lo:hi, :], p_ref[0:n_lat, :])
        if has_ctx:
            ot = ot + _dot(vct_ref[lo:hi, :], p_ref[n_lat:n_keys, :])
        outs.append(ot / l)
    o_ref[...] = jnp.concatenate(outs, axis=0).T.astype(BF16)


def _attention(q, k, vt, seq, kc=None, vct=None, ctx_len=None, cast=()):
    t = q.shape[0]
    tq = min(ATTN_Q_TILE, seq)
    tiles_per_seq = seq // tq
    has_ctx = kc is not None
    steps = (t // seq) * tiles_per_seq
    in_specs = [
        pl.BlockSpec((tq, ATTN_W), lambda b, i: (b * tiles_per_seq + i, 0)),
        pl.BlockSpec((seq, K_VARIANTS_W), lambda b, i: (b, 0)),
        pl.BlockSpec((KV_W, seq), lambda b, i: (0, b)),
    ]
    args = [q, k, vt]
    out_specs = [pl.BlockSpec((tq, ATTN_W), lambda b, i: (b * tiles_per_seq + i, 0))]
    out_shape = [jax.ShapeDtypeStruct((t, ATTN_W), BF16)]
    for arr, layer, n in cast:
        slab = n // steps
        assert slab * steps == n and slab % HALO_ROWS == 0
        in_specs.append(pl.BlockSpec((slab, arr.shape[1]),
                                     lambda b, i, layer=layer: (layer * steps + b * tiles_per_seq + i, 0)))
        args.append(arr)
        out_specs.append(pl.BlockSpec((slab, arr.shape[1]), lambda b, i: (b * tiles_per_seq + i, 0)))
        out_shape.append(jax.ShapeDtypeStruct((n, arr.shape[1]), BF16))
    if has_ctx:
        in_specs += [pl.BlockSpec((ctx_len, K_VARIANTS_W), lambda b, i: (b, 0)),
                     pl.BlockSpec((KV_W, ctx_len), lambda b, i: (0, b))]
        args += [kc, vct]
    outs = pl.pallas_call(
        functools.partial(_attn_kernel, has_ctx=has_ctx, n_cast=len(cast)),
        grid=(t // seq, tiles_per_seq),
        in_specs=in_specs,
        out_specs=out_specs,
        out_shape=out_shape,
        scratch_shapes=[pltpu.VMEM((2, seq + (ctx_len if has_ctx else 0), tq), F32),
                        pltpu.VMEM((seq + (ctx_len if has_ctx else 0), tq), BF16)],
        compiler_params=_vmem_limit(56),
    )(*args)
    return outs[0], outs[1:]


def _dft_kernel(cn_ref, sn_ref, fcs_ref, o_ref):
    y = _dot(cn_ref[...], fcs_ref[:, 0:FFT_W]) - _dot(sn_ref[...], fcs_ref[:, FFT_W:2 * FFT_W])
    o_ref[...] = y.astype(BF16)


def _position_dft(cn, sn, fcs, seq):
    t = fcs.shape[0]
    tn = min(seq, 1024)
    tiles_per_seq = seq // tn
    return pl.pallas_call(
        _dft_kernel,
        grid=(tiles_per_seq, t // seq),
        in_specs=[
            pl.BlockSpec((tn, seq), lambda i, b: (i, 0)),
            pl.BlockSpec((tn, seq), lambda i, b: (i, 0)),
            pl.BlockSpec((seq, 2 * FFT_W), lambda i, b: (b, 0)),
        ],
        out_specs=pl.BlockSpec((tn, FFT_W), lambda i, b: (b * tiles_per_seq + i, 0)),
        out_shape=jax.ShapeDtypeStruct((t, FFT_W), BF16),
        compiler_params=_vmem_limit(48),
    )(cn, sn, fcs)


def _split_bf16(x):
    hi = x.astype(BF16)
    return hi, (x - hi.astype(F32)).astype(BF16)


def _merge_kernel(o_ref, gb_ref, u_ref, up_ref, un_ref, four_ref, x_ref, mod_ref, cw_ref, gbr_ref, wout_ref,
                  gpost_ref, gffn_ref, wrh_ref, wrl_ref, br_ref,
                  xn_ref, f_ref, ti_ref, tw_ref, hist_ref, *, tiles_per_seq):
    i = pl.program_id(0)
    tm = x_ref.shape[0]
    mod = mod_ref[0]
    u = u_ref[...].astype(F32)
    rowi = lax.broadcasted_iota(jnp.int32, u.shape, 0)
    first = (i % tiles_per_seq) == 0
    last = (i % tiles_per_seq) == tiles_per_seq - 1
    prev_row = jnp.where(first, 0.0, up_ref[HALO_ROWS - 1:HALO_ROWS, :].astype(F32))
    next_row = jnp.where(last, 0.0, un_ref[0:1, :].astype(F32))
    u_up = jnp.where(rowi == 0, prev_row, pltpu.roll(u, 1, axis=0))
    u_dn = jnp.where(rowi == tm - 1, next_row, pltpu.roll(u, tm - 1, axis=0))
    conv = gb_ref[...].astype(F32) * (u_up * cw_ref[0:1, :] + u * cw_ref[1:2, :] + u_dn * cw_ref[2:3, :])
    merged = jnp.concatenate(
        [_rms(o_ref[...].astype(F32)), _rms(conv), _rms(four_ref[...].astype(F32))], axis=-1) * gbr_ref[...]
    mix = _dot(merged.astype(BF16), wout_ref[...])
    xn = x_ref[...] + mod[:, 2 * D_MODEL:3 * D_MODEL] * (_rms(mix) * gpost_ref[...])
    xn_ref[...] = xn
    f = _modulated_norm(xn, gffn_ref[...], mod[:, 3 * D_MODEL:4 * D_MODEL], mod[:, 4 * D_MODEL:5 * D_MODEL])
    for j in range(ROW_CHUNKS):
        f_ref[pl.ds(j, tm, stride=ROW_CHUNKS), :] = f[:, j * LANES:(j + 1) * LANES]
    f_hi, f_lo = _split_bf16(f)
    logits = _dot(f_hi, wrh_ref[...]) + _dot(f_lo, wrh_ref[...]) + _dot(f_hi, wrl_ref[...]) + br_ref[...]
    lane = lax.broadcasted_iota(jnp.int32, logits.shape, 1)
    neg = jnp.float32(-jnp.inf)
    work = jnp.where(lane < N_EXPERTS, logits, neg)
    top_v, top_i = [], []
    for _ in range(TOP_K):
        m = jnp.max(work, axis=-1, keepdims=True)
        idx = jnp.min(jnp.where(work == m, lane, LANES), axis=-1, keepdims=True)
        top_v.append(m)
        top_i.append(idx)
        work = jnp.where(lane == idx, neg, work)
    ex = [jnp.exp(v - top_v[0]) for v in top_v]
    den = ex[0] + ex[1] + ex[2] + ex[3]
    ti = jnp.zeros(logits.shape, jnp.int32)
    tw = jnp.zeros(logits.shape, F32)
    for kk in range(TOP_K):
        ti = jnp.where(lane == kk, top_i[kk], ti)
        tw = jnp.where(lane == kk, ex[kk] / den, tw)
    ti_ref[...] = ti
    tw_ref[...] = tw
    hits = jnp.zeros(logits.shape, jnp.int32)
    for kk in range(TOP_K):
        hits = hits + (lane == top_i[kk]).astype(jnp.int32)
    hist_ref[...] = jnp.broadcast_to(jnp.sum(hits, axis=0, keepdims=True), hist_ref.shape)


def _merge(o, gb, u, four, x, mods3, mod_row_fn, conv_w, gbr, wout, gpost, gffn, wrh, wrl, br, seq):
    t = x.shape[0]
    tm = min(2 * MERGE_ROW_TILE, seq)
    tiles_per_seq = seq // tm
    sub_per_tile = tm // HALO_ROWS
    n_sub = t // HALO_ROWS
    row = lambda i: (i, 0)
    const = lambda i: (0, 0)
    in_specs = [
        pl.BlockSpec((tm, ATTN_W), row),
        pl.BlockSpec((tm, CONV_W), row),
        pl.BlockSpec((tm, CONV_W), row),
        pl.BlockSpec((HALO_ROWS, CONV_W), lambda i: (jnp.maximum(i * sub_per_tile - 1, 0), 0)),
        pl.BlockSpec((HALO_ROWS, CONV_W), lambda i: (jnp.minimum((i + 1) * sub_per_tile, n_sub - 1), 0)),
        pl.BlockSpec((tm, FFT_W), row),
        pl.BlockSpec((tm, D_MODEL), row),
        pl.BlockSpec((1, 1, N_MOD * D_MODEL), lambda i: (mod_row_fn(i // tiles_per_seq), 0, 0)),
        pl.BlockSpec((CONV_K, CONV_W), const),
        pl.BlockSpec((1, MIX_W), const),
        pl.BlockSpec((MIX_W, D_MODEL), const),
        pl.BlockSpec((1, D_MODEL), const),
        pl.BlockSpec((1, D_MODEL), const),
        pl.BlockSpec((D_MODEL, LANES), const),
        pl.BlockSpec((D_MODEL, LANES), const),
        pl.BlockSpec((1, LANES), const),
    ]
    out_shape = [
        jax.ShapeDtypeStruct((t, D_MODEL), F32),
        jax.ShapeDtypeStruct((t * ROW_CHUNKS, LANES), F32),
        jax.ShapeDtypeStruct((t, LANES), jnp.int32),
        jax.ShapeDtypeStruct((t, LANES), F32),
        jax.ShapeDtypeStruct((t // tm * SUBLANES, LANES), jnp.int32),
    ]
    out_specs = [
        pl.BlockSpec((tm, D_MODEL), row),
        pl.BlockSpec((tm * ROW_CHUNKS, LANES), row),
        pl.BlockSpec((tm, LANES), row),
        pl.BlockSpec((tm, LANES), row),
        pl.BlockSpec((SUBLANES, LANES), row),
    ]
    return pl.pallas_call(
        functools.partial(_merge_kernel, tiles_per_seq=tiles_per_seq),
        grid=(t // tm,),
        in_specs=in_specs,
        out_specs=out_specs,
        out_shape=out_shape,
        compiler_params=_vmem_limit(48),
    )(o, gb, u, u, u, four, x, mods3, conv_w, gbr, wout, gpost, gffn, wrh, wrl, br)


def _moe_kernel(cnt_ref, off_ref, idx_ref, wl_ref, src_ref, wgu_ref, bgu_ref, wdn_ref, bdn_ref, out_ref,
                xt_ref, yt_ref, pend_ref, *, tb):
    s = pl.program_id(0)
    e = pl.program_id(1)
    tmr = MOE_ROW_TILE
    ss = MOE_STAGE_STRIDE

    @pl.when(e == 0)
    def _():
        out_ref[...] = jnp.zeros(out_ref.shape, F32)
        yt_ref[...] = jnp.zeros(yt_ref.shape, F32)
        pend_ref[0] = 0
        pend_ref[1] = 0

    cnt = cnt_ref[s * N_EXPERTS + e]
    off = off_ref[s * N_EXPERTS + e]

    def scatter_pending():
        base = pend_ref[0]
        rem = pend_ref[1]
        for g in range(tmr // SCATTER_BATCH):
            rows, news = [], []
            for mi in range(g * SCATTER_BATCH, (g + 1) * SCATTER_BATCH):
                tok = jnp.where(mi < rem, idx_ref[0, 0, base + mi], tb * ROW_CHUNKS)
                wgt = wl_ref[0, 0, base + mi]
                r0 = pl.multiple_of(tok, ROW_CHUNKS)
                rows.append(r0)
                news.append(out_ref[pl.ds(r0, ROW_CHUNKS), :] + wgt * yt_ref[pl.ds(mi, ROW_CHUNKS, stride=ss), :])
            for r0, new in zip(rows, news):
                out_ref[pl.ds(r0, ROW_CHUNKS), :] = new

    def expert_tile(base, rem, rows):
        scatter_pending()
        for mi in range(rows):
            tok = idx_ref[0, 0, base + mi]
            slab = src_ref[pl.ds(pl.multiple_of(tok, ROW_CHUNKS), ROW_CHUNKS), :]
            xt_ref[pl.ds(mi, ROW_CHUNKS, stride=ss), :] = slab
        x = jnp.concatenate([xt_ref[j * ss:j * ss + rows, :] for j in range(ROW_CHUNKS)], axis=1).astype(BF16)
        gu = _dot(x, wgu_ref[0, 0]) + bgu_ref[0, 0]
        a = jnp.minimum(gu[:, :D_FF], SWIGLU_LIMIT)
        lin = jnp.clip(gu[:, D_FF:], -SWIGLU_LIMIT, SWIGLU_LIMIT)
        act = a * (1.0 / (1.0 + jnp.exp(-SWIGLU_ALPHA * a))) * (lin + 1.0)
        y = _dot(act.astype(BF16), wdn_ref[0, 0]) + bdn_ref[0, 0]
        for j in range(ROW_CHUNKS):
            yt_ref[j * ss:j * ss + rows, :] = y[:, j * LANES:(j + 1) * LANES]
        pend_ref[0] = base
        pend_ref[1] = jnp.minimum(rem, rows)

    tail = cnt % tmr
    n_full = cnt // tmr + (tail > MOE_TAIL_TILE).astype(jnp.int32)

    def full_tile(t, carry):
        expert_tile(off + t * tmr, cnt - t * tmr, tmr)
        return carry

    lax.fori_loop(0, n_full, full_tile, 0)

    @pl.when((tail > 0) & (tail <= MOE_TAIL_TILE))
    def _():
        expert_tile(off + n_full * tmr, tail, MOE_TAIL_TILE)

    @pl.when(e == N_EXPERTS - 1)
    def _():
        scatter_pending()
        pend_ref[1] = 0


def _moe_out_index(tb, tm):
    tiles_per_sb = tb // tm
    stride = (tb + MOE_PAD_TOKENS) // tm
    return lambda i: ((i // tiles_per_sb) * stride + i % tiles_per_sb, 0)


def _moe_super_block(t):
    for tb in (4096, 2048, 1024, 512):
        if t % tb == 0:
            return tb
    raise ValueError(f"token count {t} is not a multiple of {MOE_PAD_TOKENS}")


def _routing_plan(top_i, top_w, hist, tb):
    n_sb = top_i.shape[0] // tb
    n_asg = tb * TOP_K
    list_len = n_asg + MOE_ROW_TILE
    ei = top_i[:, :TOP_K].reshape(n_sb, n_asg)
    wi = top_w[:, :TOP_K].reshape(n_sb, n_asg)
    order = jnp.argsort(ei, axis=1, stable=True)
    tok = ((order // TOP_K) * ROW_CHUNKS).astype(jnp.int32)
    wl = jnp.take_along_axis(wi, order, axis=1)
    cnt = jnp.sum(hist.reshape(n_sb, -1, LANES), axis=1)[:, :N_EXPERTS]
    off = jnp.cumsum(cnt, axis=1) - cnt
    tok = jnp.pad(tok, ((0, 0), (0, list_len - n_asg))).reshape(n_sb, 1, list_len)
    wl = jnp.pad(wl, ((0, 0), (0, list_len - n_asg))).reshape(n_sb, 1, list_len)
    return cnt, off, tok, wl


def _moe(f_rows, plan, tb, wgu, bgu, wdn, bdn, layer):
    cnt, off, tok, wl = plan
    n_sb = cnt.shape[0]
    list_len = tok.shape[2]
    out_rows = (tb + MOE_PAD_TOKENS) * ROW_CHUNKS
    grid_spec = pltpu.PrefetchScalarGridSpec(
        num_scalar_prefetch=2,
        grid=(n_sb, N_EXPERTS),
        in_specs=[
            pl.BlockSpec((1, 1, list_len), lambda s, e, c, o: (s, 0, 0), memory_space=pltpu.SMEM),
            pl.BlockSpec((1, 1, list_len), lambda s, e, c, o: (s, 0, 0), memory_space=pltpu.SMEM),
            pl.BlockSpec((tb * ROW_CHUNKS, LANES), lambda s, e, c, o: (s, 0), pipeline_mode=pl.Buffered(1)),
            pl.BlockSpec((1, 1, D_MODEL, 2 * D_FF), lambda s, e, c, o: (layer, e, 0, 0)),
            pl.BlockSpec((1, 1, 1, 2 * D_FF), lambda s, e, c, o: (layer, e, 0, 0)),
            pl.BlockSpec((1, 1, D_FF, D_MODEL), lambda s, e, c, o: (layer, e, 0, 0)),
            pl.BlockSpec((1, 1, 1, D_MODEL), lambda s, e, c, o: (layer, e, 0, 0)),
        ],
        out_specs=pl.BlockSpec((out_rows, LANES), lambda s, e, c, o: (s, 0), pipeline_mode=pl.Buffered(1)),
        scratch_shapes=[pltpu.VMEM((ROW_CHUNKS * MOE_STAGE_STRIDE, LANES), F32)] * 2 + [pltpu.SMEM((2,), jnp.int32)],
    )
    return pl.pallas_call(
        functools.partial(_moe_kernel, tb=tb),
        grid_spec=grid_spec,
        out_shape=jax.ShapeDtypeStruct((n_sb * out_rows, LANES), F32),
        compiler_params=_vmem_limit(60),
    )(cnt.reshape(-1), off.reshape(-1), tok, wl, f_rows, wgu, bgu, wdn, bdn)


def _post_kernel(x_ref, y_ref, mod_ref, g_ref, o_ref):
    tm = x_ref.shape[0]
    mod = mod_ref[0]
    o_ref[...] = x_ref[...] + mod[:, 5 * D_MODEL:6 * D_MODEL] * (_rms(_token_major_rows(y_ref, tm)) * g_ref[...])


def _post(x, y_rows, tb, mods3, mod_row_fn, g, seq):
    t = x.shape[0]
    tm = ROW_TILE
    tiles_per_seq = seq // tm
    return pl.pallas_call(
        _post_kernel,
        grid=(t // tm,),
        in_specs=[
            pl.BlockSpec((tm, D_MODEL), lambda i: (i, 0)),
            pl.BlockSpec((tm * ROW_CHUNKS, LANES), _moe_out_index(tb, tm)),
            pl.BlockSpec((1, 1, N_MOD * D_MODEL), lambda i: (mod_row_fn(i // tiles_per_seq), 0, 0)),
            pl.BlockSpec((1, D_MODEL), lambda i: (0, 0)),
        ],
        out_specs=pl.BlockSpec((tm, D_MODEL), lambda i: (i, 0)),
        out_shape=jax.ShapeDtypeStruct((t, D_MODEL), F32),
        compiler_params=_vmem_limit(48),
    )(x, y_rows, mods3, g)


def _rope_tables(seq):
    lane = jnp.arange(LANES)
    d = lane % HEAD_DIM
    axis = d // (2 * ROT_FREQS)
    freq = d % ROT_FREQS
    first_half = (d % (2 * ROT_FREQS)) < ROT_FREQS
    inv_freq = ROPE_THETA ** (-jnp.arange(ROT_FREQS, dtype=F32) / ROT_FREQS)
    tpos = jnp.arange(seq)
    pos = jnp.where(axis[None, :] == 0, (tpos // GRID_W)[:, None], (tpos % GRID_W)[:, None]).astype(F32)
    ang = pos * inv_freq[freq][None, :]
    return jnp.cos(ang), jnp.sin(ang) * jnp.where(first_half, -1.0, 1.0)[None, :].astype(F32)


def _dft_tables(n):
    k = (jnp.arange(n, dtype=jnp.int32)[:, None] * jnp.arange(n, dtype=jnp.int32)[None, :]) % n
    ang = k.astype(F32) * (2.0 * math.pi / n)
    return jnp.cos(ang), jnp.sin(ang)


def _dft_tables_split(n):
    fine = FFT_GROUP_DIM
    coarse = n // fine
    rows = jnp.arange(n, dtype=jnp.int32)[:, None]
    ang_a = ((rows * (fine * jnp.arange(coarse, dtype=jnp.int32))[None, :]) % n).astype(F32) * (2.0 * math.pi / n)
    ang_b = ((rows * jnp.arange(fine, dtype=jnp.int32)[None, :]) % n).astype(F32) * (2.0 * math.pi / n)
    ca, sa = jnp.cos(ang_a)[:, :, None], jnp.sin(ang_a)[:, :, None]
    cb, sb = jnp.cos(ang_b)[:, None, :], jnp.sin(ang_b)[:, None, :]
    return (ca * cb - sa * sb).reshape(n, n), (sa * cb + ca * sb).reshape(n, n)


def kernel(x, c, ctx, c_ctx, w_mod, b_mod, g_pre_mix, g_post_mix, g_pre_ffn, g_post_ffn, w_in, g_q, g_k, conv_w,
           g_branch, w_out, w_router, b_router, w_gate_up, b_gate_up, w_down, b_down):
    batch, seq, _ = x.shape
    ctx_len = ctx.shape[1]
    depth = w_mod.shape[0]
    assert seq % ROW_TILE == 0 and ctx_len % ROW_TILE == 0 and batch + 1 <= MOD_ROWS_PAD

    cc = jnp.zeros((MOD_ROWS_PAD, D_MODEL), F32).at[:batch].set(c).at[batch].set(c_ctx)
    mods = _modulation(cc, w_mod, b_mod)
    mods3 = mods.reshape(depth * MOD_ROWS_PAD, 1, N_MOD * D_MODEL)

    cos_l, sin_l = _rope_tables(seq)
    cos_c, sin_c = jnp.ones((ctx_len, LANES), F32), jnp.zeros((ctx_len, LANES), F32)
    cn_l, sn_l = (a.astype(BF16) for a in _dft_tables_split(seq))
    cn_c, sn_c = (a.astype(BF16) for a in _dft_tables_split(ctx_len))
    c64, s64 = _dft_tables(FFT_GROUP_DIM)
    eye = jnp.eye(FFT_GROUPS, dtype=F32)
    bdc = jnp.kron(eye, c64).astype(BF16)
    bds = jnp.kron(eye, s64).astype(BF16)

    n_exp_rows_gu = N_EXPERTS * D_MODEL
    n_exp_rows_dn = N_EXPERTS * D_FF
    wgu_rows = w_gate_up.reshape(depth * n_exp_rows_gu, 2 * D_FF)
    wdn_rows = w_down.reshape(depth * n_exp_rows_dn, D_MODEL)
    bgu = b_gate_up.reshape(depth, N_EXPERTS, 1, 2 * D_FF)
    bdn = b_down.reshape(depth, N_EXPERTS, 1, D_MODEL)

    x_lat = x.reshape(batch * seq, D_MODEL)
    x_ctx = ctx.reshape(batch * ctx_len, D_MODEL)
    post_lat = post_ctx = None
    for layer in range(depth):
        last = layer == depth - 1
        lat_row = lambda b, layer=layer: layer * MOD_ROWS_PAD + b
        ctx_row = lambda b, layer=layer: layer * MOD_ROWS_PAD + batch
        w_in_l = w_in[layer].astype(BF16)
        gpre = g_pre_mix[layer].reshape(1, D_MODEL)
        gq2 = jnp.tile(g_q[layer], 2).reshape(1, LANES)
        gk2 = jnp.tile(g_k[layer], 2).reshape(1, LANES)
        gbr = g_branch[layer].reshape(1, MIX_W)
        wout = w_out[layer].astype(BF16)
        gpost = g_post_mix[layer].reshape(1, D_MODEL)
        gffn = g_pre_ffn[layer].reshape(1, D_MODEL)
        gpf = g_post_ffn[layer].reshape(1, D_MODEL)
        wr = jnp.zeros((D_MODEL, LANES), F32).at[:, :N_EXPERTS].set(w_router[layer])
        wrh = wr.astype(BF16)
        wrl = (wr - wrh.astype(F32)).astype(BF16)
        br = jnp.zeros((1, LANES), F32).at[0, :N_EXPERTS].set(b_router[layer])

        outs = _inproj(x_lat, mods3, lat_row, gpre, w_in_l, cos_l, sin_l, gq2, gk2, bdc, bds, seq, kv_only=False,
                       post=post_lat)
        if post_lat is not None:
            x_lat, outs = outs[0], outs[1:]
        q, k_lat, vt_lat, gb, u, fcs = outs
        if last:
            k_ctx, vt_ctx = _inproj(x_ctx, mods3, ctx_row, gpre, w_in_l[:, K_OFF:CB_OFF], cos_c, sin_c, gq2, gk2,
                                    bdc, bds, ctx_len, kv_only=True, post=post_ctx)
        else:
            q_c, k_ctx, vt_ctx, gb_c, u_c, fcs_c = _inproj(x_ctx, mods3, ctx_row, gpre, w_in_l, cos_c, sin_c,
                                                           gq2, gk2, bdc, bds, ctx_len, kv_only=False)
        attn, (wgu_l, wdn_l) = _attention(q, k_lat, vt_lat, seq, k_ctx, vt_ctx, ctx_len,
                                          cast=((wgu_rows, layer, n_exp_rows_gu), (wdn_rows, layer, n_exp_rows_dn)))
        wgu_l = wgu_l.reshape(1, N_EXPERTS, D_MODEL, 2 * D_FF)
        wdn_l = wdn_l.reshape(1, N_EXPERTS, D_FF, D_MODEL)
        four = _position_dft(cn_l, sn_l, fcs, seq)
        x_lat, f_rows, ti, tw, hist = _merge(attn, gb, u, four, x_lat, mods3, lat_row, conv_w[layer], gbr, wout, gpost,
                                             gffn, wrh, wrl, br, seq)
        expert_args = (wgu_l, bgu[layer:layer + 1], wdn_l, bdn[layer:layer + 1], 0)
        tb = _moe_super_block(ti.shape[0])
        y_rows = _moe(f_rows, _routing_plan(ti, tw, hist[::SUBLANES], tb), tb, *expert_args)
        post_lat = (y_rows, tb, lat_row, gpf)
        if not last:
            attn_c, _ = _attention(q_c, k_ctx, vt_ctx, ctx_len)
            four_c = _position_dft(cn_c, sn_c, fcs_c, ctx_len)
            x_ctx, f_rows_c, ti_c, tw_c, hist_c = _merge(attn_c, gb_c, u_c, four_c, x_ctx, mods3, ctx_row,
                                                         conv_w[layer], gbr, wout, gpost, gffn, wrh, wrl, br, ctx_len)
            tb_c = _moe_super_block(ti_c.shape[0])
            y_rows_c = _moe(f_rows_c, _routing_plan(ti_c, tw_c, hist_c[::SUBLANES], tb_c), tb_c, *expert_args)
            post_ctx = (y_rows_c, tb_c, ctx_row, gpf)
    y_rows, tb, row_fn, gpf = post_lat
    x_lat = _post(x_lat, y_rows, tb, mods3, row_fn, gpf, seq)
    return x_lat.reshape(batch, seq, D_MODEL)
```
